```python
import math
import jax, jax.numpy as jnp
from jax import lax
import numpy as np

D_MODEL = 1024
BATCH = 8
SEQ = 8192
DEPTH = 1
DEC_BATCH = 8
DEC_SEQ = 16
PAST_LEN = 1024

CHUNK = 64
Q_BLOCK = 128
ROPE_THETA = 10000.0
EPS = 1e-6
A_HEADS = 8
A_HEAD_DIM = 64
A_WIDTH = A_HEADS * A_HEAD_DIM
IDX_HEADS = 8
IDX_DIM = 64
TOPK_MAX = 256
B_HEADS = 4
B_HALF_DIM = 64
B_V_DIM = 2 * B_HALF_DIM
B_QK_WIDTH = B_HEADS * 2 * B_HALF_DIM
B_V_WIDTH = B_HEADS * B_V_DIM
PEER_HEADS = 8
PEER_NKEYS = 128
PEER_N = PEER_NKEYS * PEER_NKEYS
PEER_QDIM = 256
PEER_TOPK = 16
PEER_TOKEN_BLOCK = 128
PLE_DIM = 256
SPLIT_SIZES = (A_WIDTH, A_WIDTH, A_WIDTH, IDX_HEADS * IDX_DIM, IDX_DIM, IDX_HEADS,
               B_QK_WIDTH, B_QK_WIDTH, B_V_WIDTH, 2 * D_MODEL)
IN_WIDTH = sum(SPLIT_SIZES)

kernel_name = "streaming_dsa_diffattn_peer_hybrid"


def rmsnorm(x, g):
    xf = x.astype(jnp.float32)
    y = xf * lax.rsqrt(jnp.mean(xf * xf, axis=-1, keepdims=True) + EPS)
    return (y * g.astype(jnp.float32)).astype(x.dtype)


def rope(x, pos):
    half = x.shape[-1] // 2
    inv = 1.0 / (ROPE_THETA ** (jnp.arange(half, dtype=jnp.float32) / half))
    ang = pos.astype(jnp.float32)[:, None] * inv[None, :]
    ang = ang.reshape(ang.shape[0], *([1] * (x.ndim - 3)), half)
    cos, sin = jnp.cos(ang), jnp.sin(ang)
    xf = x.astype(jnp.float32)
    x1, x2 = xf[..., :half], xf[..., half:]
    return jnp.concatenate([x1 * cos - x2 * sin, x2 * cos + x1 * sin], axis=-1).astype(x.dtype)


def project(h, w_in, b_gate, pos):
    B, T, _ = h.shape
    offs, o = [], 0
    for s in SPLIT_SIZES[:-1]:
        o += s
        offs.append(o)
    aq, ak, av, iq, ik, iw, bq, bk, bv, gates = jnp.split(h @ w_in, offs, axis=-1)
    aq = rope(aq.reshape(B, T, A_HEADS, A_HEAD_DIM), pos)
    ak = rope(ak.reshape(B, T, A_HEADS, A_HEAD_DIM), pos)
    av = av.reshape(B, T, A_HEADS, A_HEAD_DIM)
    iq = rope(iq.reshape(B, T, IDX_HEADS, IDX_DIM), pos)
    ik = rope(ik, pos)
    bq = rope(bq.reshape(B, T, B_HEADS, 2, B_HALF_DIM), pos)
    bk = rope(bk.reshape(B, T, B_HEADS, 2, B_HALF_DIM), pos)
    bv = bv.reshape(B, T, B_HEADS, B_V_DIM)
    gates = jax.nn.sigmoid(gates + b_gate)
    return aq, ak, av, iq, ik, iw, bq, bk, bv, gates


def indexer_scores(iq, iw, ik):
    s = jnp.einsum('bqhd,bkd->bqhk', iq, ik).astype(jnp.float32)
    return jnp.einsum('bqh,bqhk->bqk', iw.astype(jnp.float32), jax.nn.relu(s))


def dsa_attend(q, k, v, iq, iw, ik, mask, topk):
    score = jnp.where(mask[None], indexer_scores(iq, iw, ik), -jnp.inf)
    top_val, top_idx = lax.top_k(score, topk)
    valid = jnp.isfinite(top_val)
    gather = jax.vmap(lambda a, i: a[i])
    kg = gather(k, top_idx)
    vg = gather(v, top_idx)
    s = jnp.einsum('bqhd,bqjhd->bhqj', q, kg).astype(jnp.float32) * (A_HEAD_DIM ** -0.5)
    s = jnp.where(valid[:, None], s, -jnp.inf)
    p = jax.nn.softmax(s, axis=-1).astype(v.dtype)
    return jnp.einsum('bhqj,bqjhd->bqhd', p, vg)


def diff_attend(q, k, v, mask, lam, lam_init, g_sub):
    s = jnp.einsum('bqhcd,bkhcd->bhcqk', q, k).astype(jnp.float32) * (B_HALF_DIM ** -0.5)
    p = jax.nn.softmax(jnp.where(mask, s, -jnp.inf), axis=-1)
    a = (p[:, :, 0] - lam * p[:, :, 1]).astype(v.dtype)
    o = jnp.einsum('bhqk,bkhe->bqhe', a, v)
    return rmsnorm(o, g_sub) * (1.0 - lam_init)


def mixers_prompt(aq, ak, av, iq, ik, iw, bq, bk, bv, lam, lam_init, g_sub):
    B, S = aq.shape[:2]
    nb = S // Q_BLOCK
    topk = min(TOPK_MAX, S // 4)
    k_chunk = jnp.arange(S) // CHUNK

    def blocks(a):
        return a.reshape(B, nb, Q_BLOCK, *a.shape[2:]).swapaxes(0, 1)

    def step(args):
        bi, aq_b, iq_b, iw_b, bq_b = args
        q_chunk = (bi * Q_BLOCK + jnp.arange(Q_BLOCK)) // CHUNK
        mask = k_chunk[None, :] <= q_chunk[:, None]
        a_o = dsa_attend(aq_b, ak, av, iq_b, iw_b, ik, mask, topk)
        b_o = diff_attend(bq_b, bk, bv, mask, lam, lam_init, g_sub)
        return a_o, b_o

    a_o, b_o = lax.map(step, (jnp.arange(nb), blocks(aq), blocks(iq), blocks(iw), blocks(bq)))
    return a_o.swapaxes(0, 1).reshape(B, S, -1), b_o.swapaxes(0, 1).reshape(B, S, -1)


def mixers_sample(aq, ak, av, iq, ik, iw, bq, bk, bv, c_ak, c_av, c_ik, c_bk, c_bv, lam, lam_init, g_sub):
    B, T = aq.shape[:2]
    k_a = jnp.concatenate([c_ak, ak], axis=1)
    v_a = jnp.concatenate([c_av, av], axis=1)
    k_i = jnp.concatenate([c_ik, ik], axis=1)
    k_b = jnp.concatenate([c_bk, bk], axis=1)
    v_b = jnp.concatenate([c_bv, bv], axis=1)
    L = k_a.shape[1]
    mask = jnp.ones((T, L), dtype=bool)
    topk = min(TOPK_MAX, L // 4)
    a_o = dsa_attend(aq, k_a, v_a, iq, iw, k_i, mask, topk)
    b_o = diff_attend(bq, k_b, v_b, mask, lam, lam_init, g_sub)
    return a_o.reshape(B, T, -1), b_o.reshape(B, T, -1)


def peer_tokens(xt, w_pq, sub_k1, sub_k2, peer_u, peer_v):
    N = xt.shape[0]
    q = (xt @ w_pq).reshape(N, PEER_HEADS, 2, PEER_QDIM // 2)
    s1 = jnp.einsum('nhd,kd->nhk', q[:, :, 0], sub_k1).astype(jnp.float32)
    s2 = jnp.einsum('nhd,kd->nhk', q[:, :, 1], sub_k2).astype(jnp.float32)
    v1, i1 = lax.top_k(s1, PEER_TOPK)
    v2, i2 = lax.top_k(s2, PEER_TOPK)
    cand = (v1[..., :, None] + v2[..., None, :]).reshape(N, PEER_HEADS, PEER_TOPK * PEER_TOPK)
    cidx = (i1[..., :, None] * PEER_NKEYS + i2[..., None, :]).reshape(N, PEER_HEADS, PEER_TOPK * PEER_TOPK)
    sv, sel = lax.top_k(cand, PEER_TOPK)
    eidx = jnp.take_along_axis(cidx, sel, axis=-1)
    g = jax.nn.softmax(sv, axis=-1)
    u = peer_u[eidx]
    vv = peer_v[eidx]
    act = jax.nn.gelu(jnp.einsum('nd,nhkd->nhk', xt, u).astype(jnp.float32), approximate=False) * g
    return jnp.einsum('nhk,nhkd->nd', act.astype(xt.dtype), vv)


def merge_channel_ple(x, a_o, b_o, gates, w_pa, w_pb, w_o, g_ffn, w_pq, sub_k1, sub_k2,
                      peer_u, peer_v, g_ple, w_pg, w_ple, p, blocked):
    B, T, D = x.shape
    m = gates[..., :D] * (a_o @ w_pa) + gates[..., D:] * (b_o @ w_pb)
    x = x + m @ w_o
    ht = rmsnorm(x, g_ffn).reshape(B * T, D)
    peer = lambda t: peer_tokens(t, w_pq, sub_k1, sub_k2, peer_u, peer_v)
    if blocked:
        nblk = (B * T) // PEER_TOKEN_BLOCK
        ff = lax.map(peer, ht.reshape(nblk, PEER_TOKEN_BLOCK, D))
    else:
        ff = peer(ht)
    x = x + ff.reshape(B, T, D)
    h = rmsnorm(x, g_ple)
    return x + jax.nn.sigmoid(h @ w_pg) * (p @ w_ple)


def setup_inputs(seed: int = 0) -> dict:
    key = jax.random.key(seed)
    ks = iter(jax.random.split(key, 40))
    f32 = jnp.float32
    nrm = lambda shape, scale: jax.random.normal(next(ks), shape, f32) * scale
    gain = lambda shape: 1.0 + 0.02 * jax.random.normal(next(ks), shape, f32)
    return {
        "x_prompt": nrm((BATCH, SEQ, D_MODEL), 1.0),
        "x_sample": nrm((DEC_BATCH, DEC_SEQ, D_MODEL), 1.0),
        "cache_dsa_k": nrm((DEPTH, DEC_BATCH, PAST_LEN, A_HEADS, A_HEAD_DIM), 1.0),
        "cache_dsa_v": nrm((DEPTH, DEC_BATCH, PAST_LEN, A_HEADS, A_HEAD_DIM), 1.0),
        "cache_idx_k": nrm((DEPTH, DEC_BATCH, PAST_LEN, IDX_DIM), 1.0),
        "cache_diff_k": nrm((DEPTH, DEC_BATCH, PAST_LEN, B_HEADS, 2, B_HALF_DIM), 1.0),
        "cache_diff_v": nrm((DEPTH, DEC_BATCH, PAST_LEN, B_HEADS, B_V_DIM), 1.0),
        "p_prompt": nrm((DEPTH, BATCH, SEQ, PLE_DIM), 1.0),
        "p_sample": nrm((DEPTH, DEC_BATCH, DEC_SEQ, PLE_DIM), 1.0),
        "g_mix": gain((DEPTH, D_MODEL)),
        "w_in": nrm((DEPTH, D_MODEL, IN_WIDTH), D_MODEL ** -0.5),
        "b_gate": nrm((DEPTH, 2 * D_MODEL), 0.01),
        "lam_q1": nrm((DEPTH, B_HALF_DIM), 0.1),
        "lam_k1": nrm((DEPTH, B_HALF_DIM), 0.1),
        "lam_q2": nrm((DEPTH, B_HALF_DIM), 0.1),
        "lam_k2": nrm((DEPTH, B_HALF_DIM), 0.1),
        "g_sub": gain((DEPTH, B_V_DIM)),
        "w_pa": nrm((DEPTH, A_WIDTH, D_MODEL), A_WIDTH ** -0.5),
        "w_pb": nrm((DEPTH, B_V_WIDTH, D_MODEL), B_V_WIDTH ** -0.5),
        "w_o": nrm((DEPTH, D_MODEL, D_MODEL), D_MODEL ** -0.5),
        "g_ffn": gain((DEPTH, D_MODEL)),
        "w_pq": nrm((DEPTH, D_MODEL, PEER_HEADS * PEER_QDIM), D_MODEL ** -0.5),
        "sub_k1": nrm((DEPTH, PEER_NKEYS, PEER_QDIM // 2), (PEER_QDIM // 2) ** -0.5),
        "sub_k2": nrm((DEPTH, PEER_NKEYS, PEER_QDIM // 2), (PEER_QDIM // 2) ** -0.5),
        "peer_u": nrm((DEPTH, PEER_N, D_MODEL), D_MODEL ** -0.5),
        "peer_v": nrm((DEPTH, PEER_N, D_MODEL), PEER_HEADS ** -0.5),
        "g_ple": gain((DEPTH, D_MODEL)),
        "w_pg": nrm((DEPTH, D_MODEL, D_MODEL), D_MODEL ** -0.5),
        "w_ple": nrm((DEPTH, PLE_DIM, D_MODEL), PLE_DIM ** -0.5),
        "g_final": gain((D_MODEL,)),
    }


def reference(x_prompt, x_sample, cache_dsa_k, cache_dsa_v, cache_idx_k, cache_diff_k, cache_diff_v,
              p_prompt, p_sample, g_mix, w_in, b_gate, lam_q1, lam_k1, lam_q2, lam_k2, g_sub,
              w_pa, w_pb, w_o, g_ffn, w_pq, sub_k1, sub_k2, peer_u, peer_v, g_ple, w_pg, w_ple, g_final):
    f32 = jnp.float32
    pos_p = jnp.arange(x_prompt.shape[1])
    pos_s = PAST_LEN + jnp.arange(x_sample.shape[1])
    xp, xs = x_prompt, x_sample
    st_p = [[], [], [], [], []]
    st_s = [[], [], [], [], []]
    for i in range(DEPTH):
        lam_init = 0.8 - 0.6 * math.exp(-0.3 * i)
        lam = (jnp.exp(jnp.sum(lam_q1[i].astype(f32) * lam_k1[i].astype(f32)))
               - jnp.exp(jnp.sum(lam_q2[i].astype(f32) * lam_k2[i].astype(f32))) + lam_init)
        aq, ak, av, iq, ik, iw, bq, bk, bv, gates = project(rmsnorm(xp, g_mix[i]), w_in[i], b_gate[i], pos_p)
        a_o, b_o = mixers_prompt(aq, ak, av, iq, ik, iw, bq, bk, bv, lam, lam_init, g_sub[i])
        xp = merge_channel_ple(xp, a_o, b_o, gates, w_pa[i], w_pb[i], w_o[i], g_ffn[i], w_pq[i], sub_k1[i],
                               sub_k2[i], peer_u[i], peer_v[i], g_ple[i], w_pg[i], w_ple[i], p_prompt[i], True)
        for lst, arr in zip(st_p, (ak, av, ik, bk, bv)):
            lst.append(arr)
        aq, ak, av, iq, ik, iw, bq, bk, bv, gates = project(rmsnorm(xs, g_mix[i]), w_in[i], b_gate[i], pos_s)
        a_o, b_o = mixers_sample(aq, ak, av, iq, ik, iw, bq, bk, bv, cache_dsa_k[i], cache_dsa_v[i],
                                 cache_idx_k[i], cache_diff_k[i], cache_diff_v[i], lam, lam_init, g_sub[i])
        xs = merge_channel_ple(xs, a_o, b_o, gates, w_pa[i], w_pb[i], w_o[i], g_ffn[i], w_pq[i], sub_k1[i],
                               sub_k2[i], peer_u[i], peer_v[i], g_ple[i], w_pg[i], w_ple[i], p_sample[i], False)
        for lst, arr in zip(st_s, (ak, av, ik, bk, bv)):
            lst.append(arr)
    y_prompt = rmsnorm(xp, g_final)
    y_sample = rmsnorm(xs, g_final)
    new_dsa_k_prompt = jnp.stack(st_p[0])
    new_dsa_v_prompt = jnp.stack(st_p[1])
    new_idx_k_prompt = jnp.stack(st_p[2])
    new_diff_k_prompt = jnp.stack(st_p[3])
    new_diff_v_prompt = jnp.stack(st_p[4])
    new_dsa_k_sample = jnp.stack(st_s[0])
    new_dsa_v_sample = jnp.stack(st_s[1])
    new_idx_k_sample = jnp.stack(st_s[2])
    new_diff_k_sample = jnp.stack(st_s[3])
    new_diff_v_sample = jnp.stack(st_s[4])
    return (y_prompt, y_sample, new_dsa_k_prompt, new_dsa_v_prompt, new_idx_k_prompt, new_diff_k_prompt,
            new_diff_v_prompt, new_dsa_k_sample, new_dsa_v_sample, new_idx_k_sample, new_diff_k_sample,
            new_diff_v_sample)
```

```python
import functools
import math

import jax
import jax.numpy as jnp
from jax import lax
from jax.experimental import pallas as pl
from jax.experimental.pallas import tpu as pltpu

F32 = jnp.float32
MXU_DTYPE = jnp.bfloat16

EPS = 1e-6
CHUNK = 64
ROPE_THETA = 10000.0
LANES = 128
A_HEADS = 8
A_HEAD_DIM = 64
IDX_HEADS = 8
IDX_DIM = 64
TOPK_MAX = 256
B_HEADS = 4
B_HALF_DIM = 64
B_V_DIM = 128
PEER_HEADS = 8
PEER_NKEYS = 128
PEER_QDIM = 256
PEER_TOPK = 16
NEG_BIG = -1e30
INT_MIN = -(2 ** 31)
VMEM_LIMIT = 56 * 1024 * 1024

_NT = (((1,), (1,)), ((), ()))


def _params(sem):
    return pltpu.CompilerParams(dimension_semantics=sem, vmem_limit_bytes=VMEM_LIMIT)


def _rms(x, g):
    return x * lax.rsqrt(jnp.mean(x * x, axis=-1, keepdims=True) + EPS) * g


def _resident(shape, index_map):
    return pl.BlockSpec(shape, index_map, pipeline_mode=pl.Buffered(1))


def _project_kernel(x_ref, g_ref, w_ref, bg_ref, cos_ref, sin_ref,
                    aq_ref, akf_ref, akb_ref, avf_ref, avb_ref, iq_ref, ikf_ref, ikb_ref, iw_ref,
                    bq_ref, bkf_ref, bkb_ref, bvf_ref, bvb_ref, gates_ref):
    x = x_ref[...]
    h = _rms(x, g_ref[...]).astype(MXU_DTYPE)
    cos = cos_ref[...]
    sin = sin_ref[...]
    lane = lax.broadcasted_iota(jnp.int32, cos.shape, 1)
    first_half = (lane % 64) < 32

    def seg(c0, n):
        return jnp.dot(h, w_ref[:, c0:c0 + n], preferred_element_type=F32)

    def rope_slab(z):
        rot = jnp.where(first_half, pltpu.roll(z, 96, 1), pltpu.roll(z, 32, 1))
        return z * cos + rot * sin

    def roped(c0, n, scale, outs):
        z = seg(c0, n)
        for j in range(n // LANES):
            r = rope_slab(z[:, j * LANES:(j + 1) * LANES])
            if scale != 1.0:
                r = r * scale
            for ref in outs:
                ref[:, j * LANES:(j + 1) * LANES] = r.astype(ref.dtype)

    def plain(c0, n, outs):
        z = seg(c0, n)
        for ref in outs:
            ref[...] = z.astype(ref.dtype)

    roped(0, 512, A_HEAD_DIM ** -0.5, [aq_ref])
    roped(512, 512, 1.0, [akf_ref, akb_ref])
    plain(1024, 512, [avf_ref, avb_ref])
    roped(1536, 512, 1.0, [iq_ref])
    zi = seg(2048, 128)
    ik = rope_slab(zi)[:, :IDX_DIM]
    ikf_ref[...] = ik
    ikb_ref[...] = ik.astype(MXU_DTYPE)
    iw_ref[...] = zi[:, IDX_DIM:IDX_DIM + IDX_HEADS]
    roped(2176, 512, B_HALF_DIM ** -0.5, [bq_ref])
    roped(2688, 512, 1.0, [bkf_ref, bkb_ref])
    plain(3200, 512, [bvf_ref, bvb_ref])
    zg = seg(3712, 2048) + bg_ref[...]
    gates_ref[...] = 1.0 / (1.0 + jnp.exp(-zg))


def _project(x2d, g_mix, w_pad, b_gate, cos_t, sin_t, tm):
    n, d = x2d.shape
    nblk_pos = cos_t.shape[0] // tm
    row = lambda w: pl.BlockSpec((tm, w), lambda i: (i, 0))
    const = lambda a: _resident(a.shape, lambda i: (0,) * a.ndim)
    pos = pl.BlockSpec((tm, LANES), lambda i: (i % nblk_pos, 0))
    out_specs, out_shape = [], []
    for width, dt in [(512, MXU_DTYPE), (512, F32), (512, MXU_DTYPE), (512, F32), (512, MXU_DTYPE),
                      (512, MXU_DTYPE), (IDX_DIM, F32), (IDX_DIM, MXU_DTYPE), (IDX_HEADS, F32),
                      (512, MXU_DTYPE), (512, F32), (512, MXU_DTYPE), (512, F32), (512, MXU_DTYPE),
                      (2 * d, F32)]:
        out_specs.append(row(width))
        out_shape.append(jax.ShapeDtypeStruct((n, width), dt))
    return pl.pallas_call(
        _project_kernel,
        grid=(n // tm,),
        in_specs=[row(d), const(g_mix), const(w_pad), const(b_gate), pos, pos],
        out_specs=out_specs,
        out_shape=out_shape,
        compiler_params=_params(("parallel",)),
        name="project",
    )(x2d, g_mix, w_pad, b_gate, cos_t, sin_t)


def _admissible(i, tq, tk, sk, causal, n_valid):
    if causal:
        nkt = (i * tq + tq + tk - 1) // tk
        row = lax.broadcasted_iota(jnp.int32, (tq, 1), 0)
        adm = ((i * tq + row) // CHUNK + 1) * CHUNK
    else:
        nkt = sk // tk
        adm = jnp.full((tq, 1), n_valid, jnp.int32)
    return nkt, adm


def _flash_update(s, m, l, acc, v):
    mn = jnp.maximum(m, jnp.max(s, axis=1, keepdims=True))
    alpha = jnp.exp(m - mn)
    p = jnp.exp(s - mn)
    l = alpha * l + jnp.sum(p, axis=1, keepdims=True)
    acc = alpha * acc + jnp.dot(p.astype(MXU_DTYPE), v, preferred_element_type=F32)
    return mn, l, acc


def _key_to_float(key):
    bits = jnp.where(key >= 0, key, key ^ jnp.int32(0x7FFFFFFF))
    return lax.bitcast_convert_type(bits, F32)


def _dsa_kernel(aq_ref, iq_ref, iw_ref, ik_ref, ak_ref, av_ref, tri_ref, o_ref, sc_ref,
                *, tq, tk, sk, causal, n_valid, topk):
    i = pl.program_id(1)
    nkt, adm = _admissible(i, tq, tk, sk, causal, n_valid)
    iq = iq_ref[0]
    iw = iw_ref[0]
    aq = aq_ref[0]

    def score_tile(kt, carry):
        k0 = pl.multiple_of(kt * tk, tk)
        kb = ik_ref[0, pl.ds(k0, tk), :]
        acc = jnp.zeros((tq, tk), F32)
        for h in range(IDX_HEADS):
            s = lax.dot_general(iq[:, h * IDX_DIM:(h + 1) * IDX_DIM], kb, _NT, preferred_element_type=F32)
            acc = acc + iw[:, h:h + 1] * jnp.maximum(s, 0.0)
        kpos = k0 + lax.broadcasted_iota(jnp.int32, (tq, tk), 1)
        sc_ref[:, pl.ds(k0, tk)] = jnp.where(kpos < adm, acc, -jnp.inf)
        return carry

    lax.fori_loop(0, nkt, score_tile, 0)

    def count(pred):
        def body(kt, acc):
            k0 = pl.multiple_of(kt * tk, tk)
            hit = jnp.where(pred(sc_ref[:, pl.ds(k0, tk)]), 1.0, 0.0)
            for j in range(tk // LANES):
                acc = acc + hit[:, j * LANES:(j + 1) * LANES]
            return acc
        acc = lax.fori_loop(0, nkt, body, jnp.zeros((tq, LANES), F32))
        return jnp.sum(acc, axis=1, keepdims=True)

    def bit_step(b, key):
        cand = key ^ lax.shift_left(jnp.int32(1), 31 - b)
        cf = _key_to_float(cand)
        c = count(lambda t: t >= cf)
        return jnp.where(c >= topk, cand, key)

    key = lax.fori_loop(0, 32, bit_step, jnp.full((tq, 1), INT_MIN, jnp.int32))
    thr = _key_to_float(key)
    thr = jnp.where(jnp.abs(thr) < jnp.finfo(F32).tiny, 0.0, thr)
    thr = jnp.where(thr != thr, -jnp.inf, thr)
    need = topk - count(lambda t: t > thr)

    def attend_tile(kt, carry):
        tie_run, ms, ls, accs = carry
        k0 = pl.multiple_of(kt * tk, tk)
        sc = sc_ref[:, pl.ds(k0, tk)]
        eq = sc == thr
        eqf = jnp.where(eq, 1.0, 0.0)
        prefix = jnp.dot(eqf.astype(MXU_DTYPE), tri_ref[...], preferred_element_type=F32) + tie_run
        sel = ((sc > thr) | (eq & (prefix <= need))) & (sc > -jnp.inf)
        tie_run = tie_run + jnp.sum(eqf, axis=1, keepdims=True)
        kb = ak_ref[0, pl.ds(k0, tk), :]
        vb = av_ref[0, pl.ds(k0, tk), :]
        ms, ls, accs = list(ms), list(ls), list(accs)
        for h in range(A_HEADS):
            hs = slice(h * A_HEAD_DIM, (h + 1) * A_HEAD_DIM)
            s = lax.dot_general(aq[:, hs], kb[:, hs], _NT, preferred_element_type=F32)
            s = jnp.where(sel, s, -jnp.inf)
            ms[h], ls[h], accs[h] = _flash_update(s, ms[h], ls[h], accs[h], vb[:, hs])
        return tie_run, tuple(ms), tuple(ls), tuple(accs)

    init = (jnp.zeros((tq, 1), F32),
            tuple(jnp.full((tq, 1), NEG_BIG, F32) for _ in range(A_HEADS)),
            tuple(jnp.zeros((tq, 1), F32) for _ in range(A_HEADS)),
            tuple(jnp.zeros((tq, A_HEAD_DIM), F32) for _ in range(A_HEADS)))
    _, _, ls, accs = lax.fori_loop(0, nkt, attend_tile, init)
    for h in range(A_HEADS):
        o_ref[0, :, h * A_HEAD_DIM:(h + 1) * A_HEAD_DIM] = (accs[h] / ls[h]).astype(o_ref.dtype)


def _dsa(aq, iq, iw, ik, ak, av, *, tq, tk, causal, n_valid, topk):
    b, s, _ = aq.shape
    sk = ak.shape[1]
    assert sk % tk == 0 and s % tq == 0 and tk >= TOPK_MAX and tk % LANES == 0
    assert not causal or (tq % CHUNK == 0 and sk == s)
    tri = (jnp.arange(tk)[:, None] <= jnp.arange(tk)[None, :]).astype(MXU_DTYPE)
    qspec = lambda w: pl.BlockSpec((1, tq, w), lambda bi, i: (bi, i, 0))
    kspec = lambda w: _resident((1, sk, w), lambda bi, i: (bi, 0, 0))
    kern = functools.partial(_dsa_kernel, tq=tq, tk=tk, sk=sk, causal=causal, n_valid=n_valid, topk=topk)
    return pl.pallas_call(
        kern,
        grid=(b, s // tq),
        in_specs=[qspec(512), qspec(512), qspec(IDX_HEADS), kspec(IDX_DIM), kspec(512), kspec(512),
                  _resident((tk, tk), lambda bi, i: (0, 0))],
        out_specs=qspec(512),
        out_shape=jax.ShapeDtypeStruct((b, s, 512), MXU_DTYPE),
        scratch_shapes=[pltpu.VMEM((tq, sk), F32)],
        compiler_params=_params(("parallel", "parallel")),
        name="dsa",
    )(aq, iq, iw, ik, ak, av, tri)


def _diff_kernel(bq_ref, bk_ref, bv_ref, lq1_ref, lk1_ref, lq2_ref, lk2_ref, gsub_ref, o_ref,
                 *, tq, tk, sk, causal, n_valid, lam_init):
    i = pl.program_id(1)
    nkt, adm = _admissible(i, tq, tk, sk, causal, n_valid)
    lam = (jnp.exp(jnp.sum(lq1_ref[...] * lk1_ref[...], axis=1, keepdims=True))
           - jnp.exp(jnp.sum(lq2_ref[...] * lk2_ref[...], axis=1, keepdims=True)) + lam_init)
    bq = bq_ref[0]
    nstream = 2 * B_HEADS

    def tile(kt, carry):
        ms, ls, accs = (list(c) for c in carry)
        k0 = pl.multiple_of(kt * tk, tk)
        kb = bk_ref[0, pl.ds(k0, tk), :]
        vb = bv_ref[0, pl.ds(k0, tk), :]
        ok = (k0 + lax.broadcasted_iota(jnp.int32, (tq, tk), 1)) < adm
        for g in range(nstream):
            gs = slice(g * B_HALF_DIM, (g + 1) * B_HALF_DIM)
            h = g // 2
            s = lax.dot_general(bq[:, gs], kb[:, gs], _NT, preferred_element_type=F32)
            s = jnp.where(ok, s, -jnp.inf)
            ms[g], ls[g], accs[g] = _flash_update(s, ms[g], ls[g], accs[g], vb[:, h * B_V_DIM:(h + 1) * B_V_DIM])
        return tuple(ms), tuple(ls), tuple(accs)

    init = (tuple(jnp.full((tq, 1), NEG_BIG, F32) for _ in range(nstream)),
            tuple(jnp.zeros((tq, 1), F32) for _ in range(nstream)),
            tuple(jnp.zeros((tq, B_V_DIM), F32) for _ in range(nstream)))
    _, ls, accs = lax.fori_loop(0, nkt, tile, init)
    for h in range(B_HEADS):
        o = accs[2 * h] / ls[2 * h] - lam * (accs[2 * h + 1] / ls[2 * h + 1])
        o = _rms(o, gsub_ref[...]) * (1.0 - lam_init)
        o_ref[0, :, h * B_V_DIM:(h + 1) * B_V_DIM] = o.astype(o_ref.dtype)


def _diff(bq, bk, bv, lq1, lk1, lq2, lk2, g_sub, *, tq, tk, causal, n_valid, lam_init):
    b, s, _ = bq.shape
    sk = bk.shape[1]
    assert sk % tk == 0 and s % tq == 0
    assert not causal or (tq % CHUNK == 0 and sk == s)
    qspec = pl.BlockSpec((1, tq, 512), lambda bi, i: (bi, i, 0))
    kspec = _resident((1, sk, 512), lambda bi, i: (bi, 0, 0))
    vec = lambda a: _resident(a.shape, lambda bi, i: (0, 0))
    kern = functools.partial(_diff_kernel, tq=tq, tk=tk, sk=sk, causal=causal, n_valid=n_valid, lam_init=lam_init)
    return pl.pallas_call(
        kern,
        grid=(b, s // tq),
        in_specs=[qspec, kspec, kspec, vec(lq1), vec(lk1), vec(lq2), vec(lk2), vec(g_sub)],
        out_specs=qspec,
        out_shape=jax.ShapeDtypeStruct((b, s, 512), MXU_DTYPE),
        compiler_params=_params(("parallel", "parallel")),
        name="diff",
    )(bq, bk, bv, lq1, lk1, lq2, lk2, g_sub)


def _merge_kernel(x_ref, ao_ref, bo_ref, gates_ref, wpa_ref, wpb_ref, wo_ref, gffn_ref, wpqt_ref,
                  x1_ref, ht_ref, qt_ref):
    d = x_ref.shape[1]
    gates = gates_ref[...]
    m = (gates[:, :d] * jnp.dot(ao_ref[...], wpa_ref[...], preferred_element_type=F32)
         + gates[:, d:] * jnp.dot(bo_ref[...], wpb_ref[...], preferred_element_type=F32))
    x1 = x_ref[...] + jnp.dot(m.astype(MXU_DTYPE), wo_ref[...], preferred_element_type=F32)
    x1_ref[...] = x1
    ht = _rms(x1, gffn_ref[...]).astype(MXU_DTYPE)
    ht_ref[...] = ht
    qt_ref[...] = lax.dot_general(wpqt_ref[...], ht, _NT, preferred_element_type=F32).astype(qt_ref.dtype)


def _merge(x2d, ao, bo, gates, w_pa, w_pb, w_o, g_ffn, w_pq_t, tm):
    n, d = x2d.shape
    qw = w_pq_t.shape[0]
    row = lambda w: pl.BlockSpec((tm, w), lambda i: (i, 0))
    const = lambda a: _resident(a.shape, lambda i: (0, 0))
    return pl.pallas_call(
        _merge_kernel,
        grid=(n // tm,),
        in_specs=[row(d), row(512), row(512), row(2 * d), const(w_pa), const(w_pb), const(w_o),
                  const(g_ffn), const(w_pq_t)],
        out_specs=[row(d), row(d), pl.BlockSpec((qw, tm), lambda i: (0, i))],
        out_shape=[jax.ShapeDtypeStruct((n, d), F32), jax.ShapeDtypeStruct((n, d), MXU_DTYPE),
                   jax.ShapeDtypeStruct((qw, n), MXU_DTYPE)],
        compiler_params=_params(("parallel",)),
        name="merge",
    )(x2d, ao, bo, gates, w_pa, w_pb, w_o, g_ffn, w_pq_t)


def _top_rows(cur, k):
    nrow, tm = cur.shape
    rid = lax.broadcasted_iota(jnp.int32, (nrow, tm), 0).astype(F32)
    arow = lax.broadcasted_iota(jnp.int32, (k, tm), 0).astype(F32)

    def body(a, carry):
        cur, rank, vals, idxs = carry
        af = a.astype(F32)
        m = jnp.max(cur, axis=0, keepdims=True)
        idx = jnp.min(jnp.where(cur == m, rid, float(nrow)), axis=0, keepdims=True)
        hit = rid == idx
        rank = jnp.where(hit, af, rank)
        cur = jnp.where(hit, -jnp.inf, cur)
        return cur, rank, jnp.where(arow == af, m, vals), jnp.where(arow == af, idx, idxs)

    init = (cur, jnp.full((nrow, tm), float(k), F32), jnp.zeros((k, tm), F32), jnp.zeros((k, tm), F32))
    _, rank, vals, idxs = lax.fori_loop(0, k, body, init)
    return vals, idxs, rank


def _peer_kernel(qt_ref, ht_ref, k1_ref, k2_ref, u_ref, vt_ref, o_ref,
                 nb_ref, c1_ref, r2_ref, e2_ref, w_ref, acc_ref, *, te):
    e = pl.program_id(1)
    half = PEER_QDIM // 2
    k = PEER_TOPK
    tm = w_ref.shape[1]

    @pl.when(e == 0)
    def _route():
        acc_ref[...] = jnp.zeros_like(acc_ref)

        def head(h, carry):
            r0 = pl.multiple_of(h * PEER_QDIM, PEER_QDIM)
            s1 = jnp.dot(k1_ref[...], qt_ref[pl.ds(r0, half), :], preferred_element_type=F32)
            s2 = jnp.dot(k2_ref[...], qt_ref[pl.ds(r0 + half, half), :], preferred_element_type=F32)
            v1, _, rank1 = _top_rows(s1, k)
            v2, _, rank2 = _top_rows(s2, k)
            cand = jnp.concatenate([v1[a:a + 1, :] + v2 for a in range(k)], axis=0)
            sv, flat, _ = _top_rows(cand, k)
            arow = lax.broadcasted_iota(jnp.int32, (k, tm), 0).astype(F32)
            a_of = jnp.floor(flat * (1.0 / k))
            nb = jnp.zeros((k, tm), F32)
            for j in range(k):
                nb = nb + jnp.where(arow == a_of[j:j + 1, :], 1.0, 0.0)
            z = jnp.sum(jnp.exp(sv - sv[0:1, :]), axis=0, keepdims=True)
            nbrow = jnp.zeros_like(s1)
            for a in range(k):
                nbrow = jnp.where(rank1 == float(a), nb[a:a + 1, :], nbrow)
            nb_ref[h] = nbrow
            c1_ref[h] = jnp.exp(s1 - v1[0:1, :]) / z
            r2_ref[h] = rank2
            e2_ref[h] = jnp.exp(s2 - v2[0:1, :])
            return carry

        lax.fori_loop(0, PEER_HEADS, head, 0)

    nblk = te // PEER_NKEYS

    def wblock(j, carry):
        i1 = e * nblk + j
        w = jnp.zeros((PEER_NKEYS, tm), F32)
        for h in range(PEER_HEADS):
            nbr = nb_ref[h, pl.ds(i1, 1), :]
            c1r = c1_ref[h, pl.ds(i1, 1), :]
            w = w + jnp.where(r2_ref[h] < nbr, e2_ref[h] * c1r, 0.0)
        w_ref[pl.ds(pl.multiple_of(j * PEER_NKEYS, PEER_NKEYS), PEER_NKEYS), :] = w
        return carry

    lax.fori_loop(0, nblk, wblock, 0)
    hid = lax.dot_general(u_ref[...], ht_ref[...], _NT, preferred_element_type=F32)
    act = 0.5 * hid * (1.0 + lax.erf(hid * (2.0 ** -0.5))) * w_ref[...]
    acc_ref[...] += jnp.dot(vt_ref[...], act.astype(MXU_DTYPE), preferred_element_type=F32)

    @pl.when(e == pl.num_programs(1) - 1)
    def _emit():
        o_ref[...] = acc_ref[...].T


def _peer(qt, ht, sub_k1, sub_k2, u, vt, tm, te):
    n, d = ht.shape
    ne = u.shape[0]
    assert ne == PEER_NKEYS * PEER_NKEYS and ne % te == 0 and te % PEER_NKEYS == 0 and n % tm == 0
    const = lambda a: _resident(a.shape, lambda i, e: (0, 0))
    route = pltpu.VMEM((PEER_HEADS, PEER_NKEYS, tm), F32)
    return pl.pallas_call(
        functools.partial(_peer_kernel, te=te),
        grid=(n // tm, ne // te),
        in_specs=[pl.BlockSpec((qt.shape[0], tm), lambda i, e: (0, i)),
                  pl.BlockSpec((tm, d), lambda i, e: (i, 0)),
                  const(sub_k1), const(sub_k2),
                  pl.BlockSpec((te, d), lambda i, e: (e, 0)),
                  pl.BlockSpec((d, te), lambda i, e: (0, e))],
        out_specs=pl.BlockSpec((tm, d), lambda i, e: (i, 0)),
        out_shape=jax.ShapeDtypeStruct((n, d), F32),
        scratch_shapes=[route, route, route, route, pltpu.VMEM((te, tm), F32), pltpu.VMEM((d, tm), F32)],
        compiler_params=_params(("parallel", "arbitrary")),
        name="peer",
    )(qt, ht, sub_k1, sub_k2, u, vt)


def _final_kernel(x1_ref, ff_ref, p_ref, gple_ref, wpg_ref, wple_ref, gfin_ref, y_ref):
    x2 = x1_ref[...] + ff_ref[...]
    h = _rms(x2, gple_ref[...]).astype(MXU_DTYPE)
    gate = 1.0 / (1.0 + jnp.exp(-jnp.dot(h, wpg_ref[...], preferred_element_type=F32)))
    x3 = x2 + gate * jnp.dot(p_ref[...].astype(MXU_DTYPE), wple_ref[...], preferred_element_type=F32)
    y_ref[...] = _rms(x3, gfin_ref[...])


def _final(x1, ff, p2d, g_ple, w_pg, w_ple, g_final, tm):
    n, d = x1.shape
    row = lambda w: pl.BlockSpec((tm, w), lambda i: (i, 0))
    const = lambda a: _resident(a.shape, lambda i: (0, 0))
    return pl.pallas_call(
        _final_kernel,
        grid=(n // tm,),
        in_specs=[row(d), row(d), row(p2d.shape[1]), const(g_ple), const(w_pg), const(w_ple), const(g_final)],
        out_specs=row(d),
        out_shape=jax.ShapeDtypeStruct((n, d), F32),
        compiler_params=_params(("parallel",)),
        name="final",
    )(x1, ff, p2d, g_ple, w_pg, w_ple, g_final)


def _rope_tables(pos):
    half = A_HEAD_DIM // 2
    inv = 1.0 / (ROPE_THETA ** (jnp.arange(half, dtype=F32) / half))
    ang = pos.astype(F32)[:, None] * inv[None, :]
    cos, sin = jnp.cos(ang), jnp.sin(ang)
    return jnp.tile(cos, (1, 4)), jnp.tile(jnp.concatenate([-sin, sin], axis=1), (1, 2))


def _pad_w_in(w_in):
    d = w_in.shape[0]
    head = w_in[:, :2048]
    ikw = w_in[:, 2048:2048 + IDX_DIM + IDX_HEADS]
    pad = jnp.zeros((d, LANES - IDX_DIM - IDX_HEADS), w_in.dtype)
    return jnp.concatenate([head, ikw, pad, w_in[:, 2048 + IDX_DIM + IDX_HEADS:]], axis=1).astype(MXU_DTYPE)


def _blocks(n_tokens):
    big = n_tokens >= 4096
    return dict(tm_proj=256 if big else 128, tm_merge=256 if big else 128, tm_final=256 if big else 128,
                tm_peer=512 if big else 128, te_peer=1024)


def _layer(x, p, caches, pos, weights, lam_init):
    (g_mix, w_pad, b_gate, lq1, lk1, lq2, lk2, g_sub, w_pa, w_pb, w_o, g_ffn, w_pq_t, sub_k1, sub_k2,
     peer_u, peer_vt, g_ple, w_pg, w_ple, g_final) = weights
    b, t, d = x.shape
    n = b * t
    blk = _blocks(n)
    x2d = x.reshape(n, d)
    cos_t, sin_t = _rope_tables(pos)
    if caches is not None:
        cos_t, sin_t = jnp.tile(cos_t, (b, 1)), jnp.tile(sin_t, (b, 1))
    (aq, akf, akb, avf, avb, iq, ikf, ikb, iw, bq, bkf, bkb, bvf, bvb, gates) = _project(
        x2d, g_mix, w_pad, b_gate, cos_t, sin_t, blk["tm_proj"])
    r3 = lambda a: a.reshape(b, t, a.shape[-1])
    if caches is None:
        keys = dict(ik=r3(ikb), ak=r3(akb), av=r3(avb), bk=r3(bkb), bv=r3(bvb))
        causal, n_valid, tq, tk = True, t, 128, 512
    else:
        c_ak, c_av, c_ik, c_bk, c_bv = caches
        past = c_ak.shape[1]
        n_valid = past + t
        tk = 512
        sk = -(-n_valid // tk) * tk

        def cat(c, new):
            c = c.reshape(b, past, -1).astype(MXU_DTYPE)
            z = jnp.zeros((b, sk - n_valid, c.shape[-1]), MXU_DTYPE)
            return jnp.concatenate([c, r3(new), z], axis=1)

        keys = dict(ik=cat(c_ik, ikb), ak=cat(c_ak, akb), av=cat(c_av, avb), bk=cat(c_bk, bkb), bv=cat(c_bv, bvb))
        causal, tq = False, t
    topk = min(TOPK_MAX, n_valid // 4)
    a_o = _dsa(r3(aq), r3(iq), r3(iw), keys["ik"], keys["ak"], keys["av"],
               tq=tq, tk=tk, causal=causal, n_valid=n_valid, topk=topk)
    b_o = _diff(r3(bq), keys["bk"], keys["bv"], lq1, lk1, lq2, lk2, g_sub,
                tq=tq, tk=tk, causal=causal, n_valid=n_valid, lam_init=lam_init)
    x1, ht, qt = _merge(x2d, a_o.reshape(n, -1), b_o.reshape(n, -1), gates, w_pa, w_pb, w_o, g_ffn, w_pq_t,
                        blk["tm_merge"])
    ff = _peer(qt, ht, sub_k1, sub_k2, peer_u, peer_vt, blk["tm_peer"], blk["te_peer"])
    y = _final(x1, ff, p.reshape(n, -1), g_ple, w_pg, w_ple, g_final, blk["tm_final"])
    new = (akf.reshape(b, t, A_HEADS, A_HEAD_DIM), avf.reshape(b, t, A_HEADS, A_HEAD_DIM),
           ikf.reshape(b, t, IDX_DIM), bkf.reshape(b, t, B_HEADS, 2, B_HALF_DIM),
           bvf.reshape(b, t, B_HEADS, B_V_DIM))
    return y.reshape(b, t, d), new


def kernel(x_prompt, x_sample, cache_dsa_k, cache_dsa_v, cache_idx_k, cache_diff_k, cache_diff_v,
           p_prompt, p_sample, g_mix, w_in, b_gate, lam_q1, lam_k1, lam_q2, lam_k2, g_sub,
           w_pa, w_pb, w_o, g_ffn, w_pq, sub_k1, sub_k2, peer_u, peer_v, g_ple, w_pg, w_ple, g_final):
    depth = w_in.shape[0]
    assert depth == 1, "final norm is fused into the layer's last stage"
    i = 0
    lam_init = 0.8 - 0.6 * math.exp(-0.3 * i)
    c = lambda a: a.astype(MXU_DTYPE)
    vec = lambda a: a.reshape(1, -1)
    weights = (vec(g_mix[i]), _pad_w_in(w_in[i]), vec(b_gate[i]), vec(lam_q1[i]), vec(lam_k1[i]),
               vec(lam_q2[i]), vec(lam_k2[i]), vec(g_sub[i]), c(w_pa[i]), c(w_pb[i]), c(w_o[i]),
               vec(g_ffn[i]), c(w_pq[i].T), c(sub_k1[i]), c(sub_k2[i]), c(peer_u[i]), c(peer_v[i].T),
               vec(g_ple[i]), c(w_pg[i]), c(w_ple[i]), vec(g_final))
    past = cache_dsa_k.shape[2]
    pos_p = jnp.arange(x_prompt.shape[1])
    pos_s = past + jnp.arange(x_sample.shape[1])
    y_p, new_p = _layer(x_prompt, p_prompt[i], None, pos_p, weights, lam_init)
    caches = (cache_dsa_k[i], cache_dsa_v[i], cache_idx_k[i], cache_diff_k[i], cache_diff_v[i])
    y_s, new_s = _layer(x_sample, p_sample[i], caches, pos_s, weights, lam_init)
    stack = lambda a: a[None]
    return (y_p, y_s) + tuple(stack(a) for a in new_p) + tuple(stack(a) for a in new_s)
```

```python
import functools
import math

import jax
import jax.numpy as jnp
from jax import lax
from jax.experimental import pallas as pl
from jax.experimental.pallas import tpu as pltpu

F32 = jnp.float32
MXU_DTYPE = jnp.bfloat16
HI_DTYPE = jnp.bfloat16

EPS = 1e-6
CHUNK = 64
ROPE_THETA = 10000.0
LANES = 128
A_HEADS = 8
A_HEAD_DIM = 64
IDX_HEADS = 8
IDX_DIM = 64
TOPK_MAX = 256
B_HEADS = 4
B_HALF_DIM = 64
B_V_DIM = 128
PEER_HEADS = 8
PEER_NKEYS = 128
PEER_QDIM = 256
PEER_TOPK = 16
NEG_BIG = -1e30
VMEM_LIMIT = 56 * 1024 * 1024

_NT = (((1,), (1,)), ((), ()))


def _params(sem):
    return pltpu.CompilerParams(dimension_semantics=sem, vmem_limit_bytes=VMEM_LIMIT)


def _rms(x, g):
    return x * lax.rsqrt(jnp.mean(x * x, axis=-1, keepdims=True) + EPS) * g


def _resident(shape, index_map):
    return pl.BlockSpec(shape, index_map, pipeline_mode=pl.Buffered(1))


def _any(pred):
    return jnp.max(jnp.where(pred, 1.0, 0.0)) > 0.5


def _project_kernel(x_ref, g_ref, w_ref, bg_ref, cos_ref, sin_ref,
                    aq_ref, akf_ref, akb_ref, avf_ref, avt_ref, iq_ref, ikf_ref, ikb_ref, iwt_ref,
                    bq_ref, bkf_ref, bkb_ref, bvf_ref, bvt_ref, gates_ref):
    x = x_ref[...]
    h = _rms(x, g_ref[...]).astype(MXU_DTYPE)
    cos = cos_ref[...]
    sin = sin_ref[...]
    lane = lax.broadcasted_iota(jnp.int32, cos.shape, 1)
    first_half = (lane % 64) < 32

    def seg(c0, n):
        return jnp.dot(h, w_ref[:, c0:c0 + n], preferred_element_type=F32)

    def rope_slab(z):
        rot = jnp.where(first_half, pltpu.roll(z, 96, 1), pltpu.roll(z, 32, 1))
        return z * cos + rot * sin

    def roped(c0, n, scale, outs):
        z = seg(c0, n)
        for j in range(n // LANES):
            r = rope_slab(z[:, j * LANES:(j + 1) * LANES])
            if scale != 1.0:
                r = r * scale
            for ref in outs:
                ref[:, j * LANES:(j + 1) * LANES] = r.astype(ref.dtype)

    def value(c0, n, f_ref, t_ref):
        z = seg(c0, n)
        f_ref[...] = z
        t_ref[...] = z.T.astype(t_ref.dtype)

    roped(0, 512, A_HEAD_DIM ** -0.5, [aq_ref])
    roped(512, 512, 1.0, [akf_ref, akb_ref])
    value(1024, 512, avf_ref, avt_ref)
    roped(1536, 512, 1.0, [iq_ref])
    zi = seg(2048, 128)
    ik = rope_slab(zi)[:, :IDX_DIM]
    ikf_ref[...] = ik
    ikb_ref[...] = ik.astype(MXU_DTYPE)
    iwt_ref[...] = zi.T[IDX_DIM:IDX_DIM + IDX_HEADS, :]
    roped(2176, 512, B_HALF_DIM ** -0.5, [bq_ref])
    roped(2688, 512, 1.0, [bkf_ref, bkb_ref])
    value(3200, 512, bvf_ref, bvt_ref)
    zg = seg(3712, 2048) + bg_ref[...]
    gates_ref[...] = 1.0 / (1.0 + jnp.exp(-zg))


def _project(x2d, g_mix, w_pad, b_gate, cos_t, sin_t, tm):
    n, d = x2d.shape
    nblk_pos = cos_t.shape[0] // tm
    row = lambda w: pl.BlockSpec((tm, w), lambda i: (i, 0))
    col = lambda w: pl.BlockSpec((w, tm), lambda i: (0, i))
    const = lambda a: _resident(a.shape, lambda i: (0,) * a.ndim)
    pos = pl.BlockSpec((tm, LANES), lambda i: (i % nblk_pos, 0))
    out_specs, out_shape = [], []
    for width, dt, transposed in [
            (512, MXU_DTYPE, False), (512, F32, False), (512, MXU_DTYPE, False), (512, F32, False),
            (512, MXU_DTYPE, True), (512, MXU_DTYPE, False), (IDX_DIM, F32, False), (IDX_DIM, MXU_DTYPE, False),
            (IDX_HEADS, F32, True), (512, MXU_DTYPE, False), (512, F32, False), (512, MXU_DTYPE, False),
            (512, F32, False), (512, MXU_DTYPE, True), (2 * d, F32, False)]:
        out_specs.append(col(width) if transposed else row(width))
        out_shape.append(jax.ShapeDtypeStruct((width, n) if transposed else (n, width), dt))
    return pl.pallas_call(
        _project_kernel,
        grid=(n // tm,),
        in_specs=[row(d), const(g_mix), const(w_pad), const(b_gate), pos, pos],
        out_specs=out_specs,
        out_shape=out_shape,
        compiler_params=_params(("parallel",)),
        name="project",
    )(x2d, g_mix, w_pad, b_gate, cos_t, sin_t)


def _admissible(i, tq, tk, sk, causal, n_valid):
    if causal:
        nkt = (i * tq + tq + tk - 1) // tk
        nfull = (i * tq + CHUNK) // tk
        q = lax.broadcasted_iota(jnp.int32, (1, tq), 1)
        adm = ((i * tq + q) // CHUNK + 1) * CHUNK
    else:
        nkt = sk // tk
        nfull = n_valid // tk
        adm = jnp.full((1, tq), n_valid, jnp.int32)
    return nkt, nfull, adm


def _flash_update(s, m, l, acc, vt):
    mn = jnp.maximum(m, jnp.max(s, axis=0, keepdims=True))
    alpha = jnp.exp(m - mn)
    p = jnp.exp(s - mn)
    l = alpha * l + jnp.sum(p, axis=0, keepdims=True)
    acc = alpha * acc + jnp.dot(vt, p.astype(MXU_DTYPE), preferred_element_type=F32)
    return mn, l, acc


def _padded_queries(q_ref, qpad_ref, nhead):
    qt = q_ref[0].astype(F32).T
    upper = lax.broadcasted_iota(jnp.int32, (LANES, qt.shape[1]), 0) < 64
    for h in range(nhead):
        pair = qt[(h // 2) * LANES:(h // 2 + 1) * LANES, :]
        qpad_ref[h] = jnp.where(upper if h % 2 == 0 else ~upper, pair, 0.0).astype(qpad_ref.dtype)


def _key_to_float(key):
    bits = jnp.where(key >= 0, key, key ^ jnp.int32(0x7FFFFFFF))
    return lax.bitcast_convert_type(bits, F32)


def _float_to_key(x):
    bits = lax.bitcast_convert_type(x, jnp.int32)
    return bits ^ (lax.shift_right_arithmetic(bits, 31) & jnp.int32(0x7FFFFFFF))


def _key16_to_float(key16):
    bits16 = jnp.where(key16 >= 0, key16, key16 ^ jnp.int32(0x7FFF)) & jnp.int32(0xFFFF)
    return lax.bitcast_convert_type(lax.shift_left(bits16, 16), F32)


def _nan_to_neg_inf(x):
    return jnp.where(x != x, -jnp.inf, x)


def _dsa_kernel(aq_ref, iq_ref, iwt_ref, ik_ref, ak_ref, avt_ref, tri_ref, o_ref,
                sc_ref, hi_ref, iqt_ref, qpad_ref, s_ref, *, tq, tk, sk, causal, n_valid, topk):
    i = pl.program_id(1)
    nkt, _, adm = _admissible(i, tq, tk, sk, causal, n_valid)
    iqt_ref[...] = iq_ref[0].astype(F32).T.astype(iqt_ref.dtype)
    _padded_queries(aq_ref, qpad_ref, A_HEADS)
    iwt = iwt_ref[0]

    def tile_start(kt):
        return pl.multiple_of(kt * tk, tk)

    def score_tile(kt, carry):
        k0 = tile_start(kt)
        kb = ik_ref[0, pl.ds(k0, tk), :]
        acc = jnp.zeros((tk, tq), F32)
        for h in range(IDX_HEADS):
            s = jnp.dot(kb, iqt_ref[h * IDX_DIM:(h + 1) * IDX_DIM, :], preferred_element_type=F32)
            acc = acc + iwt[h:h + 1, :] * jnp.maximum(s, 0.0)
        kpos = k0 + lax.broadcasted_iota(jnp.int32, (tk, tq), 0)
        sc = jnp.where(kpos < adm, acc, -jnp.inf)
        sc_ref[pl.ds(k0, tk), :] = sc
        top_bits = lax.bitcast_convert_type(sc, jnp.int32) & jnp.int32(-65536)
        hi_ref[pl.ds(k0, tk), :] = lax.bitcast_convert_type(top_bits, F32).astype(HI_DTYPE)
        return carry

    lax.fori_loop(0, nkt, score_tile, 0)

    one_h, zero_h = jnp.ones((), HI_DTYPE), jnp.zeros((), HI_DTYPE)

    def fold(hit, rows):
        parts = [hit[j * rows:(j + 1) * rows] for j in range(tk // rows)]
        while len(parts) > 1:
            parts = [a + b for a, b in zip(parts[0::2], parts[1::2])]
        return parts[0]

    def count_hi(c):
        def body(kt, acc):
            hit = jnp.where(hi_ref[pl.ds(tile_start(kt), tk), :] >= c, one_h, zero_h)
            return acc + fold(hit, 16).astype(F32)
        acc = lax.fori_loop(0, nkt, body, jnp.zeros((16, tq), F32))
        return jnp.sum(acc, axis=0, keepdims=True)

    def count(pred):
        def body(kt, acc):
            hit = jnp.where(pred(sc_ref[pl.ds(tile_start(kt), tk), :]), 1.0, 0.0)
            return acc + fold(hit, 8)
        acc = lax.fori_loop(0, nkt, body, jnp.zeros((8, tq), F32))
        return jnp.sum(acc, axis=0, keepdims=True)

    def hi_step(b, carry):
        u, cnt = carry
        cand = u | lax.shift_left(jnp.int32(1), 15 - b)
        c = count_hi(_key16_to_float(cand - 32768).astype(HI_DTYPE))
        ok = c >= topk
        return jnp.where(ok, cand, u), jnp.where(ok, c, cnt)

    scanned = jnp.zeros((1, tq), F32) + jnp.asarray(nkt * tk, F32)
    u, cnt = lax.fori_loop(0, 16, hi_step, (jnp.zeros((1, tq), jnp.int32), scanned))
    thr_hi = _nan_to_neg_inf(_key16_to_float(u - 32768))
    key0 = _float_to_key(thr_hi) & jnp.int32(-65536)

    def lo_cond(carry):
        b, _, cnt = carry
        return (b < 16) & _any(cnt != topk)

    def lo_step(carry):
        b, key, cnt = carry
        cand = key | lax.shift_left(jnp.int32(1), 15 - b)
        cf = _key_to_float(cand)
        c = count(lambda t: t >= cf)
        ok = c >= topk
        return b + 1, jnp.where(ok, cand, key), jnp.where(ok, c, cnt)

    _, key, cnt = lax.while_loop(lo_cond, lo_step, (jnp.int32(0), key0, cnt))
    thr = _key_to_float(key)
    thr = jnp.where(jnp.abs(thr) < jnp.finfo(F32).tiny, 0.0, thr)
    thr = _nan_to_neg_inf(thr)
    has_ties = _any((cnt != topk) & (thr > -jnp.inf))
    need = lax.cond(has_ties, lambda: topk - count(lambda t: t > thr), lambda: jnp.zeros((1, tq), F32))

    def bias_plain(sc, tie_run):
        return jnp.where((sc >= thr) & (sc > -jnp.inf), 0.0, -jnp.inf), tie_run

    def bias_ties(sc, tie_run):
        eq = sc == thr
        eqb = jnp.where(eq, 1.0, 0.0).astype(MXU_DTYPE)
        prefix = jnp.dot(tri_ref[...], eqb, preferred_element_type=F32) + tie_run
        sel = ((sc > thr) | (eq & (prefix <= need))) & (sc > -jnp.inf)
        return jnp.where(sel, 0.0, -jnp.inf), prefix[tk - 1:tk, :]

    def attend_tile(kt, carry):
        tie_run, ms, ls, accs = carry
        k0 = tile_start(kt)
        bias, tie_run = lax.cond(has_ties, bias_ties, bias_plain, sc_ref[pl.ds(k0, tk), :], tie_run)
        ms, ls, accs = list(ms), list(ls), list(accs)
        for h in range(A_HEADS):
            kp = ak_ref[0, pl.ds(k0, tk), (h // 2) * LANES:(h // 2 + 1) * LANES]
            s_ref[h] = jnp.dot(kp, qpad_ref[h], preferred_element_type=F32) + bias
        for h in range(A_HEADS):
            vt = avt_ref[h * A_HEAD_DIM:(h + 1) * A_HEAD_DIM, pl.ds(k0, tk)]
            ms[h], ls[h], accs[h] = _flash_update(s_ref[h], ms[h], ls[h], accs[h], vt)
        return tie_run, tuple(ms), tuple(ls), tuple(accs)

    init = (jnp.zeros((1, tq), F32),
            tuple(jnp.full((1, tq), NEG_BIG, F32) for _ in range(A_HEADS)),
            tuple(jnp.zeros((1, tq), F32) for _ in range(A_HEADS)),
            tuple(jnp.zeros((A_HEAD_DIM, tq), F32) for _ in range(A_HEADS)))
    _, _, ls, accs = lax.fori_loop(0, nkt, attend_tile, init)
    ot = jnp.concatenate([accs[h] / ls[h] for h in range(A_HEADS)], axis=0)
    o_ref[0] = ot.T.astype(o_ref.dtype)


def _dsa(aq, iq, iwt, ik, ak, avt, *, tq, tk, causal, n_valid, topk):
    b, s, _ = aq.shape
    sk = ak.shape[1]
    assert sk % tk == 0 and s % tq == 0 and tk >= TOPK_MAX and tk % LANES == 0 and tq % LANES == 0
    assert tk // 16 <= 256, "per-position hit counts must stay exact in HI_DTYPE"
    assert not causal or (tq % CHUNK == 0 and sk == s)
    tri =(jnp.arange(tk)[None, :] <= jnp.arange(tk)[:, None]).astype(MXU_DTYPE)
    qspec = pl.BlockSpec((1, tq, 512), lambda bi, i: (bi, i, 0))
    kspec = lambda w: _resident((1, sk, w), lambda bi, i: (bi, 0, 0))
    kern = functools.partial(_dsa_kernel, tq=tq, tk=tk, sk=sk, causal=causal, n_valid=n_valid, topk=topk)
    return pl.pallas_call(
        kern,
        grid=(b, s // tq),
        in_specs=[qspec, qspec, pl.BlockSpec((1, IDX_HEADS, tq), lambda bi, i: (bi, 0, i)),
                  kspec(IDX_DIM), kspec(512), _resident((512, sk), lambda bi, i: (0, bi)),
                  _resident((tk, tk), lambda bi, i: (0, 0))],
        out_specs=qspec,
        out_shape=jax.ShapeDtypeStruct((b, s, 512), MXU_DTYPE),
        scratch_shapes=[pltpu.VMEM((sk, tq), F32), pltpu.VMEM((sk, tq), HI_DTYPE),
                        pltpu.VMEM((512, tq), MXU_DTYPE), pltpu.VMEM((A_HEADS, LANES, tq), MXU_DTYPE),
                        pltpu.VMEM((A_HEADS, tk, tq), F32)],
        compiler_params=_params(("parallel", "parallel")),
        name="dsa",
    )(aq, iq, iwt, ik, ak, avt, tri)


def _diff_kernel(bq_ref, bk_ref, bvt_ref, lq1_ref, lk1_ref, lq2_ref, lk2_ref, gsub_ref, o_ref, qpad_ref, s_ref,
                 *, tq, tk, sk, causal, n_valid, lam_init):
    i = pl.program_id(1)
    nkt, nfull, adm = _admissible(i, tq, tk, sk, causal, n_valid)
    lam = (jnp.exp(jnp.sum(lq1_ref[...] * lk1_ref[...], axis=1, keepdims=True))
           - jnp.exp(jnp.sum(lq2_ref[...] * lk2_ref[...], axis=1, keepdims=True)) + lam_init)
    nstream = 2 * B_HEADS
    _padded_queries(bq_ref, qpad_ref, nstream)

    def tile(kt, carry, masked):
        ms, ls, accs = (list(c) for c in carry)
        k0 = pl.multiple_of(kt * tk, tk)
        if masked:
            kpos = k0 + lax.broadcasted_iota(jnp.int32, (tk, tq), 0)
            bias = jnp.where(kpos < adm, 0.0, -jnp.inf)
        for g in range(nstream):
            h = g // 2
            s = jnp.dot(bk_ref[0, pl.ds(k0, tk), h * LANES:(h + 1) * LANES], qpad_ref[g], preferred_element_type=F32)
            s_ref[g] = s + bias if masked else s
        for g in range(nstream):
            h = g // 2
            vt = bvt_ref[h * B_V_DIM:(h + 1) * B_V_DIM, pl.ds(k0, tk)]
            ms[g], ls[g], accs[g] = _flash_update(s_ref[g], ms[g], ls[g], accs[g], vt)
        return tuple(ms), tuple(ls), tuple(accs)

    carry = (tuple(jnp.full((1, tq), NEG_BIG, F32) for _ in range(nstream)),
             tuple(jnp.zeros((1, tq), F32) for _ in range(nstream)),
             tuple(jnp.zeros((B_V_DIM, tq), F32) for _ in range(nstream)))
    carry = lax.fori_loop(0, nfull, functools.partial(tile, masked=False), carry)
    _, ls, accs = lax.fori_loop(nfull, nkt, functools.partial(tile, masked=True), carry)
    outs = []
    for h in range(B_HEADS):
        o = accs[2 * h] / ls[2 * h] - lam * (accs[2 * h + 1] / ls[2 * h + 1])
        o = o * lax.rsqrt(jnp.mean(o * o, axis=0, keepdims=True) + EPS) * gsub_ref[...]
        outs.append(o * (1.0 - lam_init))
    o_ref[0] = jnp.concatenate(outs, axis=0).T.astype(o_ref.dtype)


def _diff(bq, bk, bvt, lq1, lk1, lq2, lk2, g_sub_col, *, tq, tk, causal, n_valid, lam_init):
    b, s, _ = bq.shape
    sk = bk.shape[1]
    assert sk % tk == 0 and s % tq == 0 and tq % LANES == 0
    assert not causal or (tq % CHUNK == 0 and sk == s)
    qspec = pl.BlockSpec((1, tq, 512), lambda bi, i: (bi, i, 0))
    vec = lambda a: _resident(a.shape, lambda bi, i: (0, 0))
    kern = functools.partial(_diff_kernel, tq=tq, tk=tk, sk=sk, causal=causal, n_valid=n_valid, lam_init=lam_init)
    return pl.pallas_call(
        kern,
        grid=(b, s // tq),
        in_specs=[qspec, _resident((1, sk, 512), lambda bi, i: (bi, 0, 0)),
                  _resident((512, sk), lambda bi, i: (0, bi)),
                  vec(lq1), vec(lk1), vec(lq2), vec(lk2), vec(g_sub_col)],
        out_specs=qspec,
        out_shape=jax.ShapeDtypeStruct((b, s, 512), MXU_DTYPE),
        scratch_shapes=[pltpu.VMEM((2 * B_HEADS, LANES, tq), MXU_DTYPE), pltpu.VMEM((2 * B_HEADS, tk, tq), F32)],
        compiler_params=_params(("parallel", "parallel")),
        name="diff",
    )(bq, bk, bvt, lq1, lk1, lq2, lk2, g_sub_col)


def _merge_kernel(x_ref, ao_ref, bo_ref, gates_ref, wpa_ref, wpb_ref, wo_ref, gffn_ref, wpqt_ref,
                  x1_ref, ht_ref, qt_ref):
    d = x_ref.shape[1]
    gates = gates_ref[...]
    m = (gates[:, :d] * jnp.dot(ao_ref[...], wpa_ref[...], preferred_element_type=F32)
         + gates[:, d:] * jnp.dot(bo_ref[...], wpb_ref[...], preferred_element_type=F32))
    x1 = x_ref[...] + jnp.dot(m.astype(MXU_DTYPE), wo_ref[...], preferred_element_type=F32)
    x1_ref[...] = x1
    ht = _rms(x1, gffn_ref[...]).astype(MXU_DTYPE)
    ht_ref[...] = ht
    qt_ref[...] = lax.dot_general(wpqt_ref[...], ht, _NT, preferred_element_type=F32).astype(qt_ref.dtype)


def _merge(x2d, ao, bo, gates, w_pa, w_pb, w_o, g_ffn, w_pq_t, tm):
    n, d = x2d.shape
    qw = w_pq_t.shape[0]
    row = lambda w: pl.BlockSpec((tm, w), lambda i: (i, 0))
    const = lambda a: _resident(a.shape, lambda i: (0, 0))
    return pl.pallas_call(
        _merge_kernel,
        grid=(n // tm,),
        in_specs=[row(d), row(512), row(512), row(2 * d), const(w_pa), const(w_pb), const(w_o),
                  const(g_ffn), const(w_pq_t)],
        out_specs=[row(d), row(d), pl.BlockSpec((qw, tm), lambda i: (0, i))],
        out_shape=[jax.ShapeDtypeStruct((n, d), F32), jax.ShapeDtypeStruct((n, d), MXU_DTYPE),
                   jax.ShapeDtypeStruct((qw, n), MXU_DTYPE)],
        compiler_params=_params(("parallel",)),
        name="merge",
    )(x2d, ao, bo, gates, w_pa, w_pb, w_o, g_ffn, w_pq_t)


def _top_rows(cur, k, want_rank):
    nrow, n = cur.shape
    rid = lax.broadcasted_iota(jnp.int32, (nrow, n), 0).astype(F32)
    arow = lax.broadcasted_iota(jnp.int32, (k, n), 0).astype(F32)

    def body(a, carry):
        cur, rank, vals, idxs = carry
        af = a.astype(F32)
        m = jnp.max(cur, axis=0, keepdims=True)
        idx = jnp.min(jnp.where(cur == m, rid, float(nrow)), axis=0, keepdims=True)
        hit = rid == idx
        if want_rank:
            rank = jnp.where(hit, af, rank)
        cur = jnp.where(hit, -jnp.inf, cur)
        return cur, rank, jnp.where(arow == af, m, vals), jnp.where(arow == af, idx, idxs)

    rank0 = jnp.full((nrow, n), float(k), F32) if want_rank else jnp.zeros((1, n), F32)
    init = (cur, rank0, jnp.zeros((k, n), F32), jnp.zeros((k, n), F32))
    _, rank, vals, idxs = lax.fori_loop(0, k, body, init)
    return vals, idxs, rank


def _top_distinct(curs, k):
    n = curs[0].shape[1]
    arow = lax.broadcasted_iota(jnp.int32, (k, n), 0).astype(F32)

    def body(a, carry):
        af = a.astype(F32)
        out = []
        for cur, vals in carry:
            m = jnp.max(cur, axis=0, keepdims=True)
            out.append((jnp.where(cur == m, -jnp.inf, cur), jnp.where(arow == af, m, vals)))
        return tuple(out)

    init = tuple((c, jnp.zeros((k, n), F32)) for c in curs)
    return [(vals, cur == -jnp.inf) for cur, vals in lax.fori_loop(0, k, body, init)]


def _candidate_rows(k):
    pieces, row = [], 0
    for a in range(k // 2):
        nb = -(-(k // (a + 1)) // 8) * 8
        pieces.append((row, a, 1, nb))
        row += nb
    pieces.append((row, k // 2, k - k // 2, 1))
    return pieces, row + k - k // 2


def _candidates(v1, v2, k):
    pieces, _ = _candidate_rows(k)
    parts = [v1[a:a + 1, :] + v2[0:nb, :] if na == 1 else v1[a:a + na, :] + v2[0:1, :] for _, a, na, nb in pieces]
    return jnp.concatenate(parts, axis=0)


def _taken_per_first_key(count_rows, k):
    nb = []
    for row, a, na, nbs in _candidate_rows(k)[0]:
        nb += [count_rows(row + t * nbs, row + (t + 1) * nbs) for t in range(na)]
    return nb


def _route_distinct(s1s, s2s, k):
    g = len(s1s)
    firsts = _top_distinct(list(s1s) + list(s2s), k)
    seconds = _top_distinct([_candidates(firsts[t][0], firsts[g + t][0], k) for t in range(g)], k)
    hits = lambda mask: jnp.sum(jnp.where(mask, 1.0, 0.0), axis=0, keepdims=True)
    routes, ok = [], None
    for t in range(g):
        (v1, hit1), (v2, hit2), (sv, taken) = firsts[t], firsts[g + t], seconds[t]
        nb = _taken_per_first_key(lambda lo, hi: hits(taken[lo:hi]), k)
        nbrow = jnp.zeros_like(s1s[t])
        rank2 = jnp.zeros_like(s2s[t])
        for a in range(k):
            nbrow = jnp.where(s1s[t] == v1[a:a + 1, :], nb[a], nbrow)
            rank2 = rank2 + jnp.where(v2[a:a + 1, :] > s2s[t], 1.0, 0.0)
        good = (hits(hit1) == k) & (hits(hit2) == k) & (hits(taken) == k)
        ok = good if ok is None else ok & good
        routes.append((v1[0:1, :], nbrow, v2[0:1, :], rank2, sv))
    return routes, ok


def _route_exact(s1, s2, k):
    v1, _, rank1 = _top_rows(s1, k, True)
    v2, _, rank2 = _top_rows(s2, k, True)
    sv, rows, _ = _top_rows(_candidates(v1, v2, k), k, False)
    in_rows = lambda lo, hi: jnp.sum(jnp.where((rows >= lo) & (rows < hi), 1.0, 0.0), axis=0, keepdims=True)
    nb = _taken_per_first_key(in_rows, k)
    nbrow = jnp.zeros_like(s1)
    for a in range(k):
        nbrow = jnp.where(rank1 == float(a), nb[a], nbrow)
    return v1[0:1, :], nbrow, v2[0:1, :], rank2, sv


def _peer_kernel(qt_ref, ht_ref, k1_ref, k2_ref, u_ref, vt_ref, o_ref,
                 nb_ref, c1_ref, r2_ref, e2_ref, hid_ref, act_ref, acc_ref, *, te):
    e = pl.program_id(1)
    half = PEER_QDIM // 2
    k = PEER_TOPK
    tm = act_ref.shape[1]
    wdt = act_ref.dtype

    @pl.when(e == 0)
    def _route():
        acc_ref[...] = jnp.zeros_like(acc_ref)

        def lane_block(c, carry):
            c0 = pl.multiple_of(c * LANES, LANES)

            def scores(h):
                r0 = pl.multiple_of(h * PEER_QDIM, PEER_QDIM)
                s1 = jnp.dot(k1_ref[...], qt_ref[pl.ds(r0, half), pl.ds(c0, LANES)], preferred_element_type=F32)
                s2 = jnp.dot(k2_ref[...], qt_ref[pl.ds(r0 + half, half), pl.ds(c0, LANES)],
                             preferred_element_type=F32)
                return s1, s2

            def store(h, s1, s2, top1, nbrow, top2, rank2, sv):
                z = jnp.sum(jnp.exp(sv - sv[0:1, :]), axis=0, keepdims=True)
                nb_ref[h, :, pl.ds(c0, LANES)] = nbrow
                c1_ref[h, :, pl.ds(c0, LANES)] = jnp.exp(s1 - top1) / z
                r2_ref[h, :, pl.ds(c0, LANES)] = rank2.astype(wdt)
                e2_ref[h, :, pl.ds(c0, LANES)] = jnp.exp(s2 - top2).astype(wdt)

            def head(h, carry2):
                s1, s2 = scores(h)
                routes, ok = _route_distinct([s1], [s2], k)
                all_ok = jnp.logical_not(_any(jnp.logical_not(ok)))
                res = lax.cond(all_ok, lambda: routes[0], lambda: _route_exact(s1, s2, k))
                store(h, s1, s2, *res)
                return carry2

            return lax.fori_loop(0, PEER_HEADS, head, carry)

        lax.fori_loop(0, tm // LANES, lane_block, 0)

    nblk = te // PEER_NKEYS
    zero = jnp.zeros((), wdt)
    hid_ref[...] = lax.dot_general(u_ref[...], ht_ref[...], _NT, preferred_element_type=F32)
    for j in range(nblk):
        i1 = e * nblk + j
        rows = slice(j * PEER_NKEYS, (j + 1) * PEER_NKEYS)
        nbr = [nb_ref[h, pl.ds(i1, 1), :].astype(wdt) for h in range(PEER_HEADS)]
        c1r = [c1_ref[h, pl.ds(i1, 1), :].astype(wdt) for h in range(PEER_HEADS)]
        for c in range(tm // LANES):
            cols = slice(c * LANES, (c + 1) * LANES)
            w = None
            for h in range(PEER_HEADS):
                term = jnp.where(r2_ref[h, :, cols] < nbr[h][:, cols], e2_ref[h, :, cols] * c1r[h][:, cols], zero)
                w = term if w is None else w + term
            hj = hid_ref[rows, cols]
            gelu = 0.5 * hj * (1.0 + lax.erf(hj * (2.0 ** -0.5)))
            act_ref[rows, cols] = gelu.astype(wdt) * w
    acc_ref[...] += jnp.dot(vt_ref[...], act_ref[...], preferred_element_type=F32)

    @pl.when(e == pl.num_programs(1) - 1)
    def _emit():
        o_ref[...] = acc_ref[...].T


def _peer(qt, ht, sub_k1, sub_k2, u, vt, tm, te):
    n, d = ht.shape
    assert u.shape[0] == PEER_NKEYS * PEER_NKEYS and u.shape[0] % te == 0 and te % PEER_NKEYS == 0
    assert n % tm == 0 and tm % LANES == 0
    const = lambda a: _resident(a.shape, lambda i, e: (0, 0))
    route = lambda dt: pltpu.VMEM((PEER_HEADS, PEER_NKEYS, tm), dt)
    return pl.pallas_call(
        functools.partial(_peer_kernel, te=te),
        grid=(n // tm, u.shape[0] // te),
        in_specs=[pl.BlockSpec((qt.shape[0], tm), lambda i, e: (0, i)),
                  pl.BlockSpec((tm, d), lambda i, e: (i, 0)),
                  const(sub_k1), const(sub_k2),
                  pl.BlockSpec((te, d), lambda i, e: (e, 0)),
                  pl.BlockSpec((d, te), lambda i, e: (0, e))],
        out_specs=pl.BlockSpec((tm, d), lambda i, e: (i, 0)),
        out_shape=jax.ShapeDtypeStruct((n, d), F32),
        scratch_shapes=[route(F32), route(F32), route(MXU_DTYPE), route(MXU_DTYPE),
                        pltpu.VMEM((te, tm), F32), pltpu.VMEM((te, tm), MXU_DTYPE),
                        pltpu.VMEM((d, tm), F32)],
        compiler_params=_params(("parallel", "arbitrary")),
        name="peer",
    )(qt, ht, sub_k1, sub_k2, u, vt)


def _final_kernel(x1_ref, ff_ref, p_ref, gple_ref, wpg_ref, wple_ref, gfin_ref, y_ref):
    x2 = x1_ref[...] + ff_ref[...]
    h = _rms(x2, gple_ref[...]).astype(MXU_DTYPE)
    gate = 1.0 / (1.0 + jnp.exp(-jnp.dot(h, wpg_ref[...], preferred_element_type=F32)))
    x3 = x2 + gate * jnp.dot(p_ref[...].astype(MXU_DTYPE), wple_ref[...], preferred_element_type=F32)
    y_ref[...] = _rms(x3, gfin_ref[...])


def _final(x1, ff, p2d, g_ple, w_pg, w_ple, g_final, tm):
    n, d = x1.shape
    row = lambda w: pl.BlockSpec((tm, w), lambda i: (i, 0))
    const = lambda a: _resident(a.shape, lambda i: (0, 0))
    return pl.pallas_call(
        _final_kernel,
        grid=(n // tm,),
        in_specs=[row(d), row(d), row(p2d.shape[1]), const(g_ple), const(w_pg), const(w_ple), const(g_final)],
        out_specs=row(d),
        out_shape=jax.ShapeDtypeStruct((n, d), F32),
        compiler_params=_params(("parallel",)),
        name="final",
    )(x1, ff, p2d, g_ple, w_pg, w_ple, g_final)


def _rope_tables(pos):
    half = A_HEAD_DIM // 2
    inv = 1.0 / (ROPE_THETA ** (jnp.arange(half, dtype=F32) / half))
    ang = pos.astype(F32)[:, None] * inv[None, :]
    cos, sin = jnp.cos(ang), jnp.sin(ang)
    return jnp.tile(cos, (1, 4)), jnp.tile(jnp.concatenate([-sin, sin], axis=1), (1, 2))


def _pad_w_in(w_in):
    d = w_in.shape[0]
    head = w_in[:, :2048]
    ikw = w_in[:, 2048:2048 + IDX_DIM + IDX_HEADS]
    pad = jnp.zeros((d, LANES - IDX_DIM - IDX_HEADS), w_in.dtype)
    return jnp.concatenate([head, ikw, pad, w_in[:, 2048 + IDX_DIM + IDX_HEADS:]], axis=1).astype(MXU_DTYPE)


def _blocks(n_tokens):
    big = n_tokens >= 4096
    return dict(tm_proj=256 if big else 128, tm_merge=256 if big else 128, tm_final=256 if big else 128,
                tm_peer=512 if big else 128, te_peer=1024, tq=128, tk=512)


def _layer(x, p, caches, pos, weights, lam_init):
    (g_mix, w_pad, b_gate, lq1, lk1, lq2, lk2, g_sub_col, w_pa, w_pb, w_o, g_ffn, w_pq_t, sub_k1, sub_k2,
     peer_u, peer_vt, g_ple, w_pg, w_ple, g_final) = weights
    b, t, d = x.shape
    n = b * t
    blk = _blocks(n)
    tq, tk = blk["tq"], blk["tk"]
    x2d = x.reshape(n, d)
    cos_t, sin_t = _rope_tables(pos)
    if caches is not None:
        cos_t, sin_t = jnp.tile(cos_t, (b, 1)), jnp.tile(sin_t, (b, 1))
    (aq, akf, akb, avf, avt, iq, ikf, ikb, iwt, bq, bkf, bkb, bvf, bvt, gates) = _project(
        x2d, g_mix, w_pad, b_gate, cos_t, sin_t, blk["tm_proj"])
    r3 = lambda a: a.reshape(b, t, a.shape[-1])
    iwt3 = iwt.reshape(IDX_HEADS, b, t).transpose(1, 0, 2)
    if caches is None:
        causal, n_valid, tp = True, t, t
        q3 = r3
        ik3, ak3, bk3 = r3(ikb), r3(akb), r3(bkb)
    else:
        c_ak, c_av, c_ik, c_bk, c_bv = caches
        past = c_ak.shape[1]
        causal, n_valid = False, past + t
        sk = -(-n_valid // tk) * tk
        tp = -(-t // tq) * tq

        def q3(a):
            return jnp.pad(r3(a), ((0, 0), (0, tp - t), (0, 0)))

        def cat(c, new):
            c = c.reshape(b, past, -1).astype(MXU_DTYPE)
            return jnp.pad(jnp.concatenate([c, r3(new)], axis=1), ((0, 0), (0, sk - n_valid), (0, 0)))

        def cat_t(c, new_t):
            c = c.reshape(b, past, -1).astype(MXU_DTYPE).transpose(2, 0, 1)
            full = jnp.concatenate([c, new_t.reshape(-1, b, t)], axis=2)
            return jnp.pad(full, ((0, 0), (0, 0), (0, sk - n_valid))).reshape(-1, b * sk)

        ik3, ak3, bk3 = cat(c_ik, ikb), cat(c_ak, akb), cat(c_bk, bkb)
        avt, bvt = cat_t(c_av, avt), cat_t(c_bv, bvt)
        iwt3 = jnp.pad(iwt3, ((0, 0), (0, 0), (0, tp - t)))
    topk = min(TOPK_MAX, n_valid // 4)
    a_o = _dsa(q3(aq), q3(iq), iwt3, ik3, ak3, avt, tq=tq, tk=tk, causal=causal, n_valid=n_valid, topk=topk)
    b_o = _diff(q3(bq), bk3, bvt, lq1, lk1, lq2, lk2, g_sub_col,
                tq=tq, tk=tk, causal=causal, n_valid=n_valid, lam_init=lam_init)
    a_o, b_o = a_o[:, :t].reshape(n, -1), b_o[:, :t].reshape(n, -1)
    x1, ht, qt = _merge(x2d, a_o, b_o, gates, w_pa, w_pb, w_o, g_ffn, w_pq_t, blk["tm_merge"])
    ff = _peer(qt, ht, sub_k1, sub_k2, peer_u, peer_vt, blk["tm_peer"], blk["te_peer"])
    y = _final(x1, ff, p.reshape(n, -1), g_ple, w_pg, w_ple, g_final, blk["tm_final"])
    new = (akf.reshape(b, t, A_HEADS, A_HEAD_DIM), avf.reshape(b, t, A_HEADS, A_HEAD_DIM),
           ikf.reshape(b, t, IDX_DIM), bkf.reshape(b, t, B_HEADS, 2, B_HALF_DIM),
           bvf.reshape(b, t, B_HEADS, B_V_DIM))
    return y.reshape(b, t, d), new


def kernel(x_prompt, x_sample, cache_dsa_k, cache_dsa_v, cache_idx_k, cache_diff_k, cache_diff_v,
           p_prompt, p_sample, g_mix, w_in, b_gate, lam_q1, lam_k1, lam_q2, lam_k2, g_sub,
           w_pa, w_pb, w_o, g_ffn, w_pq, sub_k1, sub_k2, peer_u, peer_v, g_ple, w_pg, w_ple, g_final):
    depth = w_in.shape[0]
    assert depth == 1, "final norm is fused into the layer's last stage"
    i = 0
    lam_init = 0.8 - 0.6 * math.exp(-0.3 * i)
    c = lambda a: a.astype(MXU_DTYPE)
    vec = lambda a: a.reshape(1, -1)
    weights = (vec(g_mix[i]), _pad_w_in(w_in[i]), vec(b_gate[i]), vec(lam_q1[i]), vec(lam_k1[i]),
               vec(lam_q2[i]), vec(lam_k2[i]), g_sub[i].reshape(-1, 1), c(w_pa[i]), c(w_pb[i]), c(w_o[i]),
               vec(g_ffn[i]), c(w_pq[i].T), c(sub_k1[i]), c(sub_k2[i]), c(peer_u[i]), c(peer_v[i].T),
               vec(g_ple[i]), c(w_pg[i]), c(w_ple[i]), vec(g_final))
    past = cache_dsa_k.shape[2]
    pos_p = jnp.arange(x_prompt.shape[1])
    pos_s = past + jnp.arange(x_sample.shape[1])
    y_p, new_p = _layer(x_prompt, p_prompt[i], None, pos_p, weights, lam_init)
    caches = (cache_dsa_k[i], cache_dsa_v[i], cache_idx_k[i], cache_diff_k[i], cache_diff_v[i])
    y_s, new_s = _layer(x_sample, p_sample[i], caches, pos_s, weights, lam_init)
    stack = lambda a: a[None]
    return (y_p, y_s) + tuple(stack(a) for a in new_p) + tuple(stack(a) for a in new_s)
```

```python
import functools
import math

import jax
import jax.numpy as jnp
from jax import lax
from jax.experimental import pallas as pl
from jax.experimental.pallas import tpu as pltpu

F32 = jnp.float32
MXU_DTYPE = jnp.bfloat16
HI_DTYPE = jnp.bfloat16

EPS = 1e-6
CHUNK = 64
ROPE_THETA = 10000.0
LANES = 128
MXU_WIDTH = 256
NUM_MXU = 2
A_HEADS = 8
A_HEAD_DIM = 64
IDX_HEADS = 8
IDX_DIM = 64
TOPK_MAX = 256
B_HEADS = 4
B_HALF_DIM = 64
B_V_DIM = 128
PEER_HEADS = 8
PEER_NKEYS = 128
PEER_QDIM = 256
PEER_TOPK = 16
NEG_BIG = -1e30
LOG2E = math.log2(math.e)
VMEM_LIMIT = 56 * 1024 * 1024

_NT = (((1,), (1,)), ((), ()))


def _params(sem):
    return pltpu.CompilerParams(dimension_semantics=sem, vmem_limit_bytes=VMEM_LIMIT)


def _rms(x, g):
    return x * lax.rsqrt(jnp.mean(x * x, axis=-1, keepdims=True) + EPS) * g


def _resident(shape, index_map):
    return pl.BlockSpec(shape, index_map, pipeline_mode=pl.Buffered(1))


def _any(pred):
    return jnp.max(jnp.where(pred, 1.0, 0.0)) > 0.5


def _project_kernel(x_ref, g_ref, w_ref, bg_ref, cos_ref, sin_ref,
                    aq_ref, akf_ref, akb_ref, avf_ref, avt_ref, iq_ref, ikf_ref, ikb_ref, iwt_ref,
                    bq_ref, bkf_ref, bkb_ref, bvf_ref, bvt_ref, gates_ref):
    x = x_ref[...]
    h = _rms(x, g_ref[...]).astype(MXU_DTYPE)
    cos = cos_ref[...]
    sin = sin_ref[...]
    lane = lax.broadcasted_iota(jnp.int32, cos.shape, 1)
    first_half = (lane % 64) < 32

    def seg(c0, n):
        return jnp.dot(h, w_ref[:, c0:c0 + n], preferred_element_type=F32)

    def rope_slab(z):
        rot = jnp.where(first_half, pltpu.roll(z, 96, 1), pltpu.roll(z, 32, 1))
        return z * cos + rot * sin

    def roped(c0, n, scale, outs):
        z = seg(c0, n)
        for j in range(n // LANES):
            r = rope_slab(z[:, j * LANES:(j + 1) * LANES])
            if scale != 1.0:
                r = r * scale
            for ref in outs:
                ref[:, j * LANES:(j + 1) * LANES] = r.astype(ref.dtype)

    def value(c0, n, f_ref, t_ref):
        z = seg(c0, n)
        f_ref[...] = z
        t_ref[...] = z.T.astype(t_ref.dtype)

    roped(0, 512, A_HEAD_DIM ** -0.5 * LOG2E, [aq_ref])
    roped(512, 512, 1.0, [akf_ref, akb_ref])
    value(1024, 512, avf_ref, avt_ref)
    roped(1536, 512, 1.0, [iq_ref])
    zi = seg(2048, 128)
    ik = rope_slab(zi)[:, :IDX_DIM]
    ikf_ref[...] = ik
    ikb_ref[...] = ik.astype(MXU_DTYPE)
    iwt_ref[...] = zi.T[IDX_DIM:IDX_DIM + IDX_HEADS, :]
    roped(2176, 512, B_HALF_DIM ** -0.5 * LOG2E, [bq_ref])
    roped(2688, 512, 1.0, [bkf_ref, bkb_ref])
    value(3200, 512, bvf_ref, bvt_ref)
    zg = seg(3712, 2048) + bg_ref[...]
    gates_ref[...] = 1.0 / (1.0 + jnp.exp(-zg))


def _project(x2d, g_mix, w_pad, b_gate, cos_t, sin_t, tm):
    n, d = x2d.shape
    nblk_pos = cos_t.shape[0] // tm
    row = lambda w: pl.BlockSpec((tm, w), lambda i: (i, 0))
    col = lambda w: pl.BlockSpec((w, tm), lambda i: (0, i))
    const = lambda a: _resident(a.shape, lambda i: (0,) * a.ndim)
    pos = pl.BlockSpec((tm, LANES), lambda i: (i % nblk_pos, 0))
    out_specs, out_shape = [], []
    for width, dt, transposed in [
            (512, MXU_DTYPE, False), (512, F32, False), (512, MXU_DTYPE, False), (512, F32, False),
            (512, MXU_DTYPE, True), (512, MXU_DTYPE, False), (IDX_DIM, F32, False), (IDX_DIM, MXU_DTYPE, False),
            (IDX_HEADS, F32, True), (512, MXU_DTYPE, False), (512, F32, False), (512, MXU_DTYPE, False),
            (512, F32, False), (512, MXU_DTYPE, True), (2 * d, F32, False)]:
        out_specs.append(col(width) if transposed else row(width))
        out_shape.append(jax.ShapeDtypeStruct((width, n) if transposed else (n, width), dt))
    return pl.pallas_call(
        _project_kernel,
        grid=(n // tm,),
        in_specs=[row(d), const(g_mix), const(w_pad), const(b_gate), pos, pos],
        out_specs=out_specs,
        out_shape=out_shape,
        compiler_params=_params(("parallel",)),
        name="project",
    )(x2d, g_mix, w_pad, b_gate, cos_t, sin_t)


def _admissible(i, tq, tk, sk, causal, n_valid):
    if causal:
        nkt = (i * tq + tq + tk - 1) // tk
        nfull = (i * tq + CHUNK) // tk
        q = lax.broadcasted_iota(jnp.int32, (1, tq), 1)
        adm = ((i * tq + q) // CHUNK + 1) * CHUNK
    else:
        nkt = sk // tk
        nfull = n_valid // tk
        adm = jnp.full((1, tq), n_valid, jnp.int32)
    return nkt, nfull, adm


def _flash_update(s, m, l, acc, vt):
    mn = jnp.maximum(m, jnp.max(s, axis=0, keepdims=True))
    alpha = jnp.exp2(m - mn)
    p = jnp.exp2(s - mn)
    l = alpha * l + jnp.sum(p, axis=0, keepdims=True)
    acc = alpha * acc + jnp.dot(vt, p.astype(MXU_DTYPE), preferred_element_type=F32)
    return mn, l, acc


def _padded_queries(q_ref, qpad_ref, nhead):
    qt = q_ref[0].astype(F32).T
    upper = lax.broadcasted_iota(jnp.int32, (LANES, qt.shape[1]), 0) < 64
    for j in range(nhead // 2):
        pair = qt[j * LANES:(j + 1) * LANES, :]
        both = jnp.concatenate([jnp.where(upper, pair, 0.0), jnp.where(upper, 0.0, pair)], axis=1)
        qpad_ref[j] = both.astype(qpad_ref.dtype)


def _key_to_float(key):
    bits = jnp.where(key >= 0, key, key ^ jnp.int32(0x7FFFFFFF))
    return lax.bitcast_convert_type(bits, F32)


def _float_to_key(x):
    bits = lax.bitcast_convert_type(x, jnp.int32)
    return bits ^ (lax.shift_right_arithmetic(bits, 31) & jnp.int32(0x7FFFFFFF))


def _key16_to_float(key16):
    bits16 = jnp.where(key16 >= 0, key16, key16 ^ jnp.int32(0x7FFF)) & jnp.int32(0xFFFF)
    return lax.bitcast_convert_type(lax.shift_left(bits16, 16), F32)


def _nan_to_neg_inf(x):
    return jnp.where(x != x, -jnp.inf, x)


def _dsa_kernel(aq_ref, iq_ref, iwt_ref, ik_ref, ak_ref, avt_ref, tri_ref, o_ref,
                sc_ref, hi_ref, iqt_ref, qpad_ref, s_ref, *, tq, tk, sk, causal, n_valid, topk):
    i = pl.program_id(1)
    nkt, _, adm = _admissible(i, tq, tk, sk, causal, n_valid)
    iqt = iq_ref[0].astype(F32).T.astype(iqt_ref.dtype)
    for j in range(IDX_HEADS // 2):
        iqt_ref[j] = jnp.concatenate([iqt[2 * j * IDX_DIM:(2 * j + 1) * IDX_DIM, :],
                                      iqt[(2 * j + 1) * IDX_DIM:(2 * j + 2) * IDX_DIM, :]], axis=1)
    _padded_queries(aq_ref, qpad_ref, A_HEADS)
    iwt = iwt_ref[0]

    def tile_start(kt):
        return pl.multiple_of(kt * tk, tk)

    def score_tile(kt, carry):
        k0 = tile_start(kt)
        kb = ik_ref[0, pl.ds(k0, tk), :]
        acc = jnp.zeros((tk, tq), F32)
        for j in range(IDX_HEADS // 2):
            s = jnp.dot(kb, iqt_ref[j], preferred_element_type=F32)
            acc = acc + iwt[2 * j:2 * j + 1, :] * jnp.maximum(s[:, :tq], 0.0)
            acc = acc + iwt[2 * j + 1:2 * j + 2, :] * jnp.maximum(s[:, tq:], 0.0)
        kpos = k0 + lax.broadcasted_iota(jnp.int32, (tk, tq), 0)
        sc = jnp.where(kpos < adm, acc, -jnp.inf)
        sc_ref[pl.ds(k0, tk), :] = sc
        top_bits = lax.bitcast_convert_type(sc, jnp.int32) & jnp.int32(-65536)
        hi_ref[pl.ds(k0, tk), :] = lax.bitcast_convert_type(top_bits, F32).astype(HI_DTYPE)
        return carry

    lax.fori_loop(0, nkt, score_tile, 0)

    one_h, zero_h = jnp.ones((), HI_DTYPE), jnp.zeros((), HI_DTYPE)

    def fold(hit, rows):
        parts = [hit[j * rows:(j + 1) * rows] for j in range(tk // rows)]
        while len(parts) > 1:
            parts = [a + b for a, b in zip(parts[0::2], parts[1::2])]
        return parts[0]

    def count_hi(c):
        def body(kt, acc):
            hit = jnp.where(hi_ref[pl.ds(tile_start(kt), tk), :] >= c, one_h, zero_h)
            return acc + fold(hit, 16).astype(F32)
        acc = lax.fori_loop(0, nkt, body, jnp.zeros((16, tq), F32))
        return jnp.sum(acc, axis=0, keepdims=True)

    def count(pred):
        def body(kt, acc):
            hit = jnp.where(pred(sc_ref[pl.ds(tile_start(kt), tk), :]), 1.0, 0.0)
            return acc + fold(hit, 8)
        acc = lax.fori_loop(0, nkt, body, jnp.zeros((8, tq), F32))
        return jnp.sum(acc, axis=0, keepdims=True)

    def hi_step(b, carry):
        u, cnt = carry
        cand = u | lax.shift_left(jnp.int32(1), 15 - b)
        c = count_hi(_key16_to_float(cand - 32768).astype(HI_DTYPE))
        ok = c >= topk
        return jnp.where(ok, cand, u), jnp.where(ok, c, cnt)

    scanned = jnp.zeros((1, tq), F32) + jnp.asarray(nkt * tk, F32)
    u, cnt = lax.fori_loop(0, 16, hi_step, (jnp.zeros((1, tq), jnp.int32), scanned))
    thr_hi = _nan_to_neg_inf(_key16_to_float(u - 32768))
    key0 = _float_to_key(thr_hi) & jnp.int32(-65536)

    def lo_cond(carry):
        b, _, cnt = carry
        return (b < 16) & _any(cnt != topk)

    def lo_step(carry):
        b, key, cnt = carry
        cand = key | lax.shift_left(jnp.int32(1), 15 - b)
        cf = _key_to_float(cand)
        c = count(lambda t: t >= cf)
        ok = c >= topk
        return b + 1, jnp.where(ok, cand, key), jnp.where(ok, c, cnt)

    _, key, cnt = lax.while_loop(lo_cond, lo_step, (jnp.int32(0), key0, cnt))
    thr = _key_to_float(key)
    thr = jnp.where(jnp.abs(thr) < jnp.finfo(F32).tiny, 0.0, thr)
    thr = _nan_to_neg_inf(thr)
    has_ties = _any((cnt != topk) & (thr > -jnp.inf))
    need = lax.cond(has_ties, lambda: topk - count(lambda t: t > thr), lambda: jnp.zeros((1, tq), F32))

    def bias_plain(sc, tie_run):
        return jnp.where((sc >= thr) & (sc > -jnp.inf), 0.0, -jnp.inf), tie_run

    def bias_ties(sc, tie_run):
        eq = sc == thr
        eqb = jnp.where(eq, 1.0, 0.0).astype(MXU_DTYPE)
        prefix = jnp.dot(tri_ref[...], eqb, preferred_element_type=F32) + tie_run
        sel = ((sc > thr) | (eq & (prefix <= need))) & (sc > -jnp.inf)
        return jnp.where(sel, 0.0, -jnp.inf), prefix[tk - 1:tk, :]

    def attend_tile(kt, carry):
        tie_run, ms, ls, accs = carry
        k0 = tile_start(kt)
        bias, tie_run = lax.cond(has_ties, bias_ties, bias_plain, sc_ref[pl.ds(k0, tk), :], tie_run)
        ms, ls, accs = list(ms), list(ls), list(accs)
        for j in range(A_HEADS // 2):
            kp = ak_ref[0, pl.ds(k0, tk), j * LANES:(j + 1) * LANES]
            s = jnp.dot(kp, qpad_ref[j], preferred_element_type=F32)
            s_ref[2 * j] = s[:, :tq] + bias
            s_ref[2 * j + 1] = s[:, tq:] + bias
        for h in range(A_HEADS):
            vt = avt_ref[h * A_HEAD_DIM:(h + 1) * A_HEAD_DIM, pl.ds(k0, tk)]
            ms[h], ls[h], accs[h] = _flash_update(s_ref[h], ms[h], ls[h], accs[h], vt)
        return tie_run, tuple(ms), tuple(ls), tuple(accs)

    init = (jnp.zeros((1, tq), F32),
            tuple(jnp.full((1, tq), NEG_BIG, F32) for _ in range(A_HEADS)),
            tuple(jnp.zeros((1, tq), F32) for _ in range(A_HEADS)),
            tuple(jnp.zeros((A_HEAD_DIM, tq), F32) for _ in range(A_HEADS)))
    _, _, ls, accs = lax.fori_loop(0, nkt, attend_tile, init)
    ot = jnp.concatenate([accs[h] / ls[h] for h in range(A_HEADS)], axis=0)
    o_ref[0] = ot.T.astype(o_ref.dtype)


def _dsa(aq, iq, iwt, ik, ak, avt, *, tq, tk, causal, n_valid, topk):
    b, s, _ = aq.shape
    sk = ak.shape[1]
    assert sk % tk == 0 and s % tq == 0 and tk >= TOPK_MAX and tk % LANES == 0 and tq % LANES == 0
    assert tk // 16 <= 256, "per-position hit counts must stay exact in HI_DTYPE"
    assert not causal or (tq % CHUNK == 0 and sk == s)
    tri =(jnp.arange(tk)[None, :] <= jnp.arange(tk)[:, None]).astype(MXU_DTYPE)
    qspec = pl.BlockSpec((1, tq, 512), lambda bi, i: (bi, i, 0))
    kspec = lambda w: _resident((1, sk, w), lambda bi, i: (bi, 0, 0))
    kern = functools.partial(_dsa_kernel, tq=tq, tk=tk, sk=sk, causal=causal, n_valid=n_valid, topk=topk)
    return pl.pallas_call(
        kern,
        grid=(b, s // tq),
        in_specs=[qspec, qspec, pl.BlockSpec((1, IDX_HEADS, tq), lambda bi, i: (bi, 0, i)),
                  kspec(IDX_DIM), kspec(512), _resident((512, sk), lambda bi, i: (0, bi)),
                  _resident((tk, tk), lambda bi, i: (0, 0))],
        out_specs=qspec,
        out_shape=jax.ShapeDtypeStruct((b, s, 512), MXU_DTYPE),
        scratch_shapes=[pltpu.VMEM((sk, tq), F32), pltpu.VMEM((sk, tq), HI_DTYPE),
                        pltpu.VMEM((IDX_HEADS // 2, IDX_DIM, 2 * tq), MXU_DTYPE),
                        pltpu.VMEM((A_HEADS // 2, LANES, 2 * tq), MXU_DTYPE),
                        pltpu.VMEM((A_HEADS, tk, tq), F32)],
        compiler_params=_params(("parallel", "parallel")),
        name="dsa",
    )(aq, iq, iwt, ik, ak, avt, tri)


def _diff_kernel(bq_ref, bk_ref, bvt_ref, lq1_ref, lk1_ref, lq2_ref, lk2_ref, gsub_ref, o_ref, qpad_ref, s_ref,
                 *, tq, tk, sk, causal, n_valid, lam_init):
    i = pl.program_id(1)
    nkt, nfull, adm = _admissible(i, tq, tk, sk, causal, n_valid)
    lam = (jnp.exp(jnp.sum(lq1_ref[...] * lk1_ref[...], axis=1, keepdims=True))
           - jnp.exp(jnp.sum(lq2_ref[...] * lk2_ref[...], axis=1, keepdims=True)) + lam_init)
    nstream = 2 * B_HEADS
    _padded_queries(bq_ref, qpad_ref, nstream)

    def tile(kt, carry, masked):
        ms, ls, accs = (list(c) for c in carry)
        k0 = pl.multiple_of(kt * tk, tk)
        if masked:
            kpos = k0 + lax.broadcasted_iota(jnp.int32, (tk, tq), 0)
            bias = jnp.where(kpos < adm, 0.0, -jnp.inf)
        for h in range(B_HEADS):
            s = jnp.dot(bk_ref[0, pl.ds(k0, tk), h * LANES:(h + 1) * LANES], qpad_ref[h], preferred_element_type=F32)
            for c in range(2):
                sc = s[:, c * tq:(c + 1) * tq]
                s_ref[2 * h + c] = sc + bias if masked else sc
        for g in range(nstream):
            h = g // 2
            vt = bvt_ref[h * B_V_DIM:(h + 1) * B_V_DIM, pl.ds(k0, tk)]
            ms[g], ls[g], accs[g] = _flash_update(s_ref[g], ms[g], ls[g], accs[g], vt)
        return tuple(ms), tuple(ls), tuple(accs)

    carry = (tuple(jnp.full((1, tq), NEG_BIG, F32) for _ in range(nstream)),
             tuple(jnp.zeros((1, tq), F32) for _ in range(nstream)),
             tuple(jnp.zeros((B_V_DIM, tq), F32) for _ in range(nstream)))
    carry = lax.fori_loop(0, nfull, functools.partial(tile, masked=False), carry)
    _, ls, accs = lax.fori_loop(nfull, nkt, functools.partial(tile, masked=True), carry)
    outs = []
    for h in range(B_HEADS):
        o = accs[2 * h] / ls[2 * h] - lam * (accs[2 * h + 1] / ls[2 * h + 1])
        o = o * lax.rsqrt(jnp.mean(o * o, axis=0, keepdims=True) + EPS) * gsub_ref[...]
        outs.append(o * (1.0 - lam_init))
    o_ref[0] = jnp.concatenate(outs, axis=0).T.astype(o_ref.dtype)


def _diff(bq, bk, bvt, lq1, lk1, lq2, lk2, g_sub_col, *, tq, tk, causal, n_valid, lam_init):
    b, s, _ = bq.shape
    sk = bk.shape[1]
    assert sk % tk == 0 and s % tq == 0 and tq % LANES == 0
    assert not causal or (tq % CHUNK == 0 and sk == s)
    qspec = pl.BlockSpec((1, tq, 512), lambda bi, i: (bi, i, 0))
    vec = lambda a: _resident(a.shape, lambda bi, i: (0, 0))
    kern = functools.partial(_diff_kernel, tq=tq, tk=tk, sk=sk, causal=causal, n_valid=n_valid, lam_init=lam_init)
    return pl.pallas_call(
        kern,
        grid=(b, s // tq),
        in_specs=[qspec, _resident((1, sk, 512), lambda bi, i: (bi, 0, 0)),
                  _resident((512, sk), lambda bi, i: (0, bi)),
                  vec(lq1), vec(lk1), vec(lq2), vec(lk2), vec(g_sub_col)],
        out_specs=qspec,
        out_shape=jax.ShapeDtypeStruct((b, s, 512), MXU_DTYPE),
        scratch_shapes=[pltpu.VMEM((B_HEADS, LANES, 2 * tq), MXU_DTYPE), pltpu.VMEM((2 * B_HEADS, tk, tq), F32)],
        compiler_params=_params(("parallel", "parallel")),
        name="diff",
    )(bq, bk, bvt, lq1, lk1, lq2, lk2, g_sub_col)


def _merge_kernel(x_ref, ao_ref, bo_ref, gates_ref, wpa_ref, wpb_ref, wo_ref, gffn_ref, wpqt_ref,
                  x1_ref, htt_ref, qt_ref):
    d = x_ref.shape[1]
    gates = gates_ref[...]
    m = (gates[:, :d] * jnp.dot(ao_ref[...], wpa_ref[...], preferred_element_type=F32)
         + gates[:, d:] * jnp.dot(bo_ref[...], wpb_ref[...], preferred_element_type=F32))
    x1 = x_ref[...] + jnp.dot(m.astype(MXU_DTYPE), wo_ref[...], preferred_element_type=F32)
    x1_ref[...] = x1
    ht = _rms(x1, gffn_ref[...])
    htt_ref[...] = ht.T.astype(MXU_DTYPE)
    qt_ref[...] = lax.dot_general(wpqt_ref[...], ht.astype(MXU_DTYPE), _NT,
                                  preferred_element_type=F32).astype(qt_ref.dtype)


def _merge(x2d, ao, bo, gates, w_pa, w_pb, w_o, g_ffn, w_pq_t, tm):
    n, d = x2d.shape
    qw = w_pq_t.shape[0]
    row = lambda w: pl.BlockSpec((tm, w), lambda i: (i, 0))
    const = lambda a: _resident(a.shape, lambda i: (0, 0))
    return pl.pallas_call(
        _merge_kernel,
        grid=(n // tm,),
        in_specs=[row(d), row(512), row(512), row(2 * d), const(w_pa), const(w_pb), const(w_o),
                  const(g_ffn), const(w_pq_t)],
        out_specs=[row(d), pl.BlockSpec((d, tm), lambda i: (0, i)), pl.BlockSpec((qw, tm), lambda i: (0, i))],
        out_shape=[jax.ShapeDtypeStruct((n, d), F32), jax.ShapeDtypeStruct((d, n), MXU_DTYPE),
                   jax.ShapeDtypeStruct((qw, n), MXU_DTYPE)],
        compiler_params=_params(("parallel",)),
        name="merge",
    )(x2d, ao, bo, gates, w_pa, w_pb, w_o, g_ffn, w_pq_t)


def _top_rows(cur, k, want_rank):
    nrow, n = cur.shape
    rid = lax.broadcasted_iota(jnp.int32, (nrow, n), 0).astype(F32)
    arow = lax.broadcasted_iota(jnp.int32, (k, n), 0).astype(F32)

    def body(a, carry):
        cur, rank, vals, idxs = carry
        af = a.astype(F32)
        m = jnp.max(cur, axis=0, keepdims=True)
        idx = jnp.min(jnp.where(cur == m, rid, float(nrow)), axis=0, keepdims=True)
        hit = rid == idx
        if want_rank:
            rank = jnp.where(hit, af, rank)
        cur = jnp.where(hit, -jnp.inf, cur)
        return cur, rank, jnp.where(arow == af, m, vals), jnp.where(arow == af, idx, idxs)

    rank0 = jnp.full((nrow, n), float(k), F32) if want_rank else jnp.zeros((1, n), F32)
    init = (cur, rank0, jnp.zeros((k, n), F32), jnp.zeros((k, n), F32))
    _, rank, vals, idxs = lax.fori_loop(0, k, body, init)
    return vals, idxs, rank


def _top_distinct(curs, k):
    n = curs[0].shape[1]
    arow = lax.broadcasted_iota(jnp.int32, (k, n), 0).astype(F32)

    def body(a, carry):
        af = a.astype(F32)
        out = []
        for cur, vals in carry:
            m = jnp.max(cur, axis=0, keepdims=True)
            out.append((jnp.where(cur == m, -jnp.inf, cur), jnp.where(arow == af, m, vals)))
        return tuple(out)

    init = tuple((c, jnp.zeros((k, n), F32)) for c in curs)
    return [(vals, cur == -jnp.inf) for cur, vals in lax.fori_loop(0, k, body, init)]


def _candidate_rows(k):
    pieces, row = [], 0
    for a in range(k // 2):
        nb = -(-(k // (a + 1)) // 8) * 8
        pieces.append((row, a, 1, nb))
        row += nb
    pieces.append((row, k // 2, k - k // 2, 1))
    return pieces, row + k - k // 2


def _candidates(v1, v2, k):
    pieces, _ = _candidate_rows(k)
    parts = [v1[a:a + 1, :] + v2[0:nb, :] if na == 1 else v1[a:a + na, :] + v2[0:1, :] for _, a, na, nb in pieces]
    return jnp.concatenate(parts, axis=0)


def _taken_per_first_key(count_rows, k):
    nb = []
    for row, a, na, nbs in _candidate_rows(k)[0]:
        nb += [count_rows(row + t * nbs, row + (t + 1) * nbs) for t in range(na)]
    return nb


def _route_distinct(s1s, s2s, k):
    g = len(s1s)
    firsts = _top_distinct(list(s1s) + list(s2s), k)
    seconds = _top_distinct([_candidates(firsts[t][0], firsts[g + t][0], k) for t in range(g)], k)
    hits = lambda mask: jnp.sum(jnp.where(mask, 1.0, 0.0), axis=0, keepdims=True)
    routes, ok = [], None
    for t in range(g):
        (v1, hit1), (v2, hit2), (sv, taken) = firsts[t], firsts[g + t], seconds[t]
        nb = _taken_per_first_key(lambda lo, hi: hits(taken[lo:hi]), k)
        nbrow = jnp.zeros_like(s1s[t])
        rank2 = jnp.zeros_like(s2s[t])
        for a in range(k):
            nbrow = jnp.where(s1s[t] == v1[a:a + 1, :], nb[a], nbrow)
            rank2 = rank2 + jnp.where(v2[a:a + 1, :] > s2s[t], 1.0, 0.0)
        good = (hits(hit1) == k) & (hits(hit2) == k) & (hits(taken) == k)
        ok = good if ok is None else ok & good
        routes.append((v1[0:1, :], nbrow, v2[0:1, :], rank2, sv))
    return routes, ok


def _route_exact(s1, s2, k):
    v1, _, rank1 = _top_rows(s1, k, True)
    v2, _, rank2 = _top_rows(s2, k, True)
    sv, rows, _ = _top_rows(_candidates(v1, v2, k), k, False)
    in_rows = lambda lo, hi: jnp.sum(jnp.where((rows >= lo) & (rows < hi), 1.0, 0.0), axis=0, keepdims=True)
    nb = _taken_per_first_key(in_rows, k)
    nbrow = jnp.zeros_like(s1)
    for a in range(k):
        nbrow = jnp.where(rank1 == float(a), nb[a], nbrow)
    return v1[0:1, :], nbrow, v2[0:1, :], rank2, sv


def _peer_kernel(qt_ref, htt_ref, k1_ref, k2_ref, u_ref, vt_ref, o_ref,
                 nb_ref, c1_ref, r2_ref, e2_ref, act_ref, acc_ref, *, te):
    e = pl.program_id(1)
    half = PEER_QDIM // 2
    k = PEER_TOPK
    tm = act_ref.shape[1]
    wdt = act_ref.dtype

    @pl.when(e == 0)
    def _route():
        acc_ref[...] = jnp.zeros_like(acc_ref)

        def lane_block(c, carry):
            c0 = pl.multiple_of(c * LANES, LANES)

            def scores(h):
                r0 = pl.multiple_of(h * PEER_QDIM, PEER_QDIM)
                s1 = jnp.dot(k1_ref[...], qt_ref[pl.ds(r0, half), pl.ds(c0, LANES)], preferred_element_type=F32)
                s2 = jnp.dot(k2_ref[...], qt_ref[pl.ds(r0 + half, half), pl.ds(c0, LANES)],
                             preferred_element_type=F32)
                return s1, s2

            def store(h, s1, s2, top1, nbrow, top2, rank2, sv):
                z = jnp.sum(jnp.exp(sv - sv[0:1, :]), axis=0, keepdims=True)
                nb_ref[h, :, pl.ds(c0, LANES)] = nbrow
                c1_ref[h, :, pl.ds(c0, LANES)] = jnp.exp(s1 - top1) / z
                r2_ref[h, :, pl.ds(c0, LANES)] = rank2.astype(wdt)
                e2_ref[h, :, pl.ds(c0, LANES)] = jnp.exp(s2 - top2).astype(wdt)

            def head(h, carry2):
                s1, s2 = scores(h)
                routes, ok = _route_distinct([s1], [s2], k)
                all_ok = jnp.logical_not(_any(jnp.logical_not(ok)))
                res = lax.cond(all_ok, lambda: routes[0], lambda: _route_exact(s1, s2, k))
                store(h, s1, s2, *res)
                return carry2

            return lax.fori_loop(0, PEER_HEADS, head, carry)

        lax.fori_loop(0, tm // LANES, lane_block, 0)

    nblk = te // PEER_NKEYS
    zero = jnp.zeros((), wdt)
    cw = min(tm, NUM_MXU * MXU_WIDTH)
    for t in range(tm // cw):
        chunk = slice(t * cw, (t + 1) * cw)
        hid = jnp.dot(u_ref[...], htt_ref[:, chunk], preferred_element_type=F32)
        for j in range(nblk):
            i1 = e * nblk + j
            rows = slice(j * PEER_NKEYS, (j + 1) * PEER_NKEYS)
            nbr = [nb_ref[h, pl.ds(i1, 1), :].astype(wdt) for h in range(PEER_HEADS)]
            c1r = [c1_ref[h, pl.ds(i1, 1), :].astype(wdt) for h in range(PEER_HEADS)]
            for c in range(cw // LANES):
                cols = slice(t * cw + c * LANES, t * cw + (c + 1) * LANES)
                w = None
                for h in range(PEER_HEADS):
                    term = jnp.where(r2_ref[h, :, cols] < nbr[h][:, cols], e2_ref[h, :, cols] * c1r[h][:, cols],
                                     zero)
                    w = term if w is None else w + term
                hj = hid[rows, c * LANES:(c + 1) * LANES]
                gelu = 0.5 * hj * (1.0 + lax.erf(hj * (2.0 ** -0.5)))
                act_ref[rows, cols] = gelu.astype(wdt) * w
        acc_ref[:, chunk] += jnp.dot(vt_ref[...], act_ref[:, chunk], preferred_element_type=F32)

    @pl.when(e == pl.num_programs(1) - 1)
    def _emit():
        o_ref[...] = acc_ref[...].T


def _peer(qt, htt, sub_k1, sub_k2, u, vt, tm, te):
    d, n = htt.shape
    assert u.shape[0] == PEER_NKEYS * PEER_NKEYS and u.shape[0] % te == 0 and te % PEER_NKEYS == 0
    assert n % tm == 0 and tm % LANES == 0
    const = lambda a: _resident(a.shape, lambda i, e: (0, 0))
    route = lambda dt: pltpu.VMEM((PEER_HEADS, PEER_NKEYS, tm), dt)
    return pl.pallas_call(
        functools.partial(_peer_kernel, te=te),
        grid=(n // tm, u.shape[0] // te),
        in_specs=[pl.BlockSpec((qt.shape[0], tm), lambda i, e: (0, i)),
                  pl.BlockSpec((d, tm), lambda i, e: (0, i)),
                  const(sub_k1), const(sub_k2),
                  pl.BlockSpec((te, d), lambda i, e: (e, 0)),
                  pl.BlockSpec((d, te), lambda i, e: (0, e))],
        out_specs=pl.BlockSpec((tm, d), lambda i, e: (i, 0)),
        out_shape=jax.ShapeDtypeStruct((n, d), F32),
        scratch_shapes=[route(F32), route(F32), route(MXU_DTYPE), route(MXU_DTYPE),
                        pltpu.VMEM((te, tm), MXU_DTYPE), pltpu.VMEM((d, tm), F32)],
        compiler_params=_params(("parallel", "arbitrary")),
        name="peer",
    )(qt, htt, sub_k1, sub_k2, u, vt)


def _final_kernel(x1_ref, ff_ref, p_ref, gple_ref, wpg_ref, wple_ref, gfin_ref, y_ref):
    x2 = x1_ref[...] + ff_ref[...]
    h = _rms(x2, gple_ref[...]).astype(MXU_DTYPE)
    gate = 1.0 / (1.0 + jnp.exp(-jnp.dot(h, wpg_ref[...], preferred_element_type=F32)))
    x3 = x2 + gate * jnp.dot(p_ref[...].astype(MXU_DTYPE), wple_ref[...], preferred_element_type=F32)
    y_ref[...] = _rms(x3, gfin_ref[...])


def _final(x1, ff, p2d, g_ple, w_pg, w_ple, g_final, tm):
    n, d = x1.shape
    row = lambda w: pl.BlockSpec((tm, w), lambda i: (i, 0))
    const = lambda a: _resident(a.shape, lambda i: (0, 0))
    return pl.pallas_call(
        _final_kernel,
        grid=(n // tm,),
        in_specs=[row(d), row(d), row(p2d.shape[1]), const(g_ple), const(w_pg), const(w_ple), const(g_final)],
        out_specs=row(d),
        out_shape=jax.ShapeDtypeStruct((n, d), F32),
        compiler_params=_params(("parallel",)),
        name="final",
    )(x1, ff, p2d, g_ple, w_pg, w_ple, g_final)


def _rope_tables(pos):
    half = A_HEAD_DIM // 2
    inv = 1.0 / (ROPE_THETA ** (jnp.arange(half, dtype=F32) / half))
    ang = pos.astype(F32)[:, None] * inv[None, :]
    cos, sin = jnp.cos(ang), jnp.sin(ang)
    return jnp.tile(cos, (1, 4)), jnp.tile(jnp.concatenate([-sin, sin], axis=1), (1, 2))


def _pad_w_in(w_in):
    d = w_in.shape[0]
    head = w_in[:, :2048]
    ikw = w_in[:, 2048:2048 + IDX_DIM + IDX_HEADS]
    pad = jnp.zeros((d, LANES - IDX_DIM - IDX_HEADS), w_in.dtype)
    return jnp.concatenate([head, ikw, pad, w_in[:, 2048 + IDX_DIM + IDX_HEADS:]], axis=1).astype(MXU_DTYPE)


def _blocks(n_tokens):
    big = n_tokens >= 4096
    return dict(tm_proj=256 if big else 128, tm_merge=256 if big else 128, tm_final=256 if big else 128,
                tm_peer=512 if big else 128, te_peer=1024, tq=256, tq_diff=512, tk=512)


def _layer(x, p, caches, pos, weights, lam_init):
    (g_mix, w_pad, b_gate, lq1, lk1, lq2, lk2, g_sub_col, w_pa, w_pb, w_o, g_ffn, w_pq_t, sub_k1, sub_k2,
     peer_u, peer_vt, g_ple, w_pg, w_ple, g_final) = weights
    b, t, d = x.shape
    n = b * t
    blk = _blocks(n)
    tq, tq_diff, tk = blk["tq"], blk["tq_diff"], blk["tk"]
    x2d = x.reshape(n, d)
    cos_t, sin_t = _rope_tables(pos)
    if caches is not None:
        cos_t, sin_t = jnp.tile(cos_t, (b, 1)), jnp.tile(sin_t, (b, 1))
    (aq, akf, akb, avf, avt, iq, ikf, ikb, iwt, bq, bkf, bkb, bvf, bvt, gates) = _project(
        x2d, g_mix, w_pad, b_gate, cos_t, sin_t, blk["tm_proj"])
    r3 = lambda a: a.reshape(b, t, a.shape[-1])
    iwt3 = iwt.reshape(IDX_HEADS, b, t).transpose(1, 0, 2)
    if caches is None:
        causal, n_valid = True, t
        q3 = lambda a, tile: r3(a)
        ik3, ak3, bk3 = r3(ikb), r3(akb), r3(bkb)
    else:
        c_ak, c_av, c_ik, c_bk, c_bv = caches
        past = c_ak.shape[1]
        causal, n_valid = False, past + t
        sk = -(-n_valid // tk) * tk

        def q3(a, tile):
            return jnp.pad(r3(a), ((0, 0), (0, -t % tile), (0, 0)))

        def cat(c, new):
            c = c.reshape(b, past, -1).astype(MXU_DTYPE)
            return jnp.pad(jnp.concatenate([c, r3(new)], axis=1), ((0, 0), (0, sk - n_valid), (0, 0)))

        def cat_t(c, new_t):
            c = c.reshape(b, past, -1).astype(MXU_DTYPE).transpose(2, 0, 1)
            full = jnp.concatenate([c, new_t.reshape(-1, b, t)], axis=2)
            return jnp.pad(full, ((0, 0), (0, 0), (0, sk - n_valid))).reshape(-1, b * sk)

        ik3, ak3, bk3 = cat(c_ik, ikb), cat(c_ak, akb), cat(c_bk, bkb)
        avt, bvt = cat_t(c_av, avt), cat_t(c_bv, bvt)
        iwt3 = jnp.pad(iwt3, ((0, 0), (0, 0), (0, -t % tq)))
    topk = min(TOPK_MAX, n_valid // 4)
    a_o = _dsa(q3(aq, tq), q3(iq, tq), iwt3, ik3, ak3, avt, tq=tq, tk=tk, causal=causal, n_valid=n_valid, topk=topk)
    b_o = _diff(q3(bq, tq_diff), bk3, bvt, lq1, lk1, lq2, lk2, g_sub_col,
                tq=tq_diff, tk=tk, causal=causal, n_valid=n_valid, lam_init=lam_init)
    a_o, b_o = a_o[:, :t].reshape(n, -1), b_o[:, :t].reshape(n, -1)
    x1, htt, qt = _merge(x2d, a_o, b_o, gates, w_pa, w_pb, w_o, g_ffn, w_pq_t, blk["tm_merge"])
    ff = _peer(qt, htt, sub_k1, sub_k2, peer_u, peer_vt, blk["tm_peer"], blk["te_peer"])
    y = _final(x1, ff, p.reshape(n, -1), g_ple, w_pg, w_ple, g_final, blk["tm_final"])
    new = (akf.reshape(b, t, A_HEADS, A_HEAD_DIM), avf.reshape(b, t, A_HEADS, A_HEAD_DIM),
           ikf.reshape(b, t, IDX_DIM), bkf.reshape(b, t, B_HEADS, 2, B_HALF_DIM),
           bvf.reshape(b, t, B_HEADS, B_V_DIM))
    return y.reshape(b, t, d), new


def kernel(x_prompt, x_sample, cache_dsa_k, cache_dsa_v, cache_idx_k, cache_diff_k, cache_diff_v,
           p_prompt, p_sample, g_mix, w_in, b_gate, lam_q1, lam_k1, lam_q2, lam_k2, g_sub,
           w_pa, w_pb, w_o, g_ffn, w_pq, sub_k1, sub_k2, peer_u, peer_v, g_ple, w_pg, w_ple, g_final):
    depth = w_in.shape[0]
    assert depth == 1, "final norm is fused into the layer's last stage"
    i = 0
    lam_init = 0.8 - 0.6 * math.exp(-0.3 * i)
    c = lambda a: a.astype(MXU_DTYPE)
    vec = lambda a: a.reshape(1, -1)
    weights = (vec(g_mix[i]), _pad_w_in(w_in[i]), vec(b_gate[i]), vec(lam_q1[i]), vec(lam_k1[i]),
               vec(lam_q2[i]), vec(lam_k2[i]), g_sub[i].reshape(-1, 1), c(w_pa[i]), c(w_pb[i]), c(w_o[i]),
               vec(g_ffn[i]), c(w_pq[i].T), c(sub_k1[i]), c(sub_k2[i]), c(peer_u[i]), c(peer_v[i].T),
               vec(g_ple[i]), c(w_pg[i]), c(w_ple[i]), vec(g_final))
    past = cache_dsa_k.shape[2]
    pos_p = jnp.arange(x_prompt.shape[1])
    pos_s = past + jnp.arange(x_sample.shape[1])
    y_p, new_p = _layer(x_prompt, p_prompt[i], None, pos_p, weights, lam_init)
    caches = (cache_dsa_k[i], cache_dsa_v[i], cache_idx_k[i], cache_diff_k[i], cache_diff_v[i])
    y_s, new_s = _layer(x_sample, p_sample[i], caches, pos_s, weights, lam_init)
    stack = lambda a: a[None]
    return (y_p, y_s) + tuple(stack(a) for a in new_p) + tuple(stack(a) for a in new_s)
```

```python
import functools
import math

import jax
import jax.numpy as jnp
from jax import lax
from jax.experimental import pallas as pl
from jax.experimental.pallas import tpu as pltpu

F32 = jnp.float32
MXU_DTYPE = jnp.bfloat16
HI_DTYPE = jnp.bfloat16

EPS = 1e-6
CHUNK = 64
ROPE_THETA = 10000.0
LANES = 128
MXU_WIDTH = 256
NUM_MXU = 2
COUNT_ACCUMULATORS = 4
SOFTMAX_ROWS = 64
A_HEADS = 8
A_HEAD_DIM = 64
IDX_HEADS = 8
IDX_DIM = 64
TOPK_MAX = 256
B_HEADS = 4
B_HALF_DIM = 64
B_V_DIM = 128
PEER_HEADS = 8
PEER_NKEYS = 128
PEER_QDIM = 256
PEER_TOPK = 16
NEG_BIG = -1e30
LOG2E = math.log2(math.e)
VMEM_LIMIT = 56 * 1024 * 1024

_NT = (((1,), (1,)), ((), ()))


def _params(sem):
    return pltpu.CompilerParams(dimension_semantics=sem, vmem_limit_bytes=VMEM_LIMIT)


def _rms(x, g):
    return x * lax.rsqrt(jnp.mean(x * x, axis=-1, keepdims=True) + EPS) * g


def _resident(shape, index_map):
    return pl.BlockSpec(shape, index_map, pipeline_mode=pl.Buffered(1))


def _any(pred):
    return jnp.max(jnp.where(pred, 1.0, 0.0)) > 0.5


def _project_kernel(x_ref, g_ref, w_ref, bg_ref, cos_ref, sin_ref,
                    aq_ref, akf_ref, akb_ref, avf_ref, avt_ref, iq_ref, ikf_ref, ikb_ref, iwt_ref,
                    bq_ref, bkf_ref, bkb_ref, bvf_ref, bvt_ref, gates_ref):
    x = x_ref[...]
    h = _rms(x, g_ref[...]).astype(MXU_DTYPE)
    cos = cos_ref[...]
    sin = sin_ref[...]
    lane = lax.broadcasted_iota(jnp.int32, cos.shape, 1)
    first_half = (lane % 64) < 32

    def seg(c0, n):
        return jnp.dot(h, w_ref[:, c0:c0 + n], preferred_element_type=F32)

    def rope_slab(z):
        rot = jnp.where(first_half, pltpu.roll(z, 96, 1), pltpu.roll(z, 32, 1))
        return z * cos + rot * sin

    def roped(c0, n, scale, outs):
        z = seg(c0, n)
        for j in range(n // LANES):
            r = rope_slab(z[:, j * LANES:(j + 1) * LANES])
            if scale != 1.0:
                r = r * scale
            for ref in outs:
                ref[:, j * LANES:(j + 1) * LANES] = r.astype(ref.dtype)

    def value(c0, n, f_ref, t_ref):
        z = seg(c0, n)
        f_ref[...] = z
        t_ref[...] = z.T.astype(t_ref.dtype)

    roped(0, 512, A_HEAD_DIM ** -0.5 * LOG2E, [aq_ref])
    roped(512, 512, 1.0, [akf_ref, akb_ref])
    value(1024, 512, avf_ref, avt_ref)
    roped(1536, 512, 1.0, [iq_ref])
    zi = seg(2048, 128)
    ik = rope_slab(zi)[:, :IDX_DIM]
    ikf_ref[...] = ik
    ikb_ref[...] = ik.astype(MXU_DTYPE)
    iwt_ref[...] = zi.T[IDX_DIM:IDX_DIM + IDX_HEADS, :]
    roped(2176, 512, B_HALF_DIM ** -0.5 * LOG2E, [bq_ref])
    roped(2688, 512, 1.0, [bkf_ref, bkb_ref])
    value(3200, 512, bvf_ref, bvt_ref)
    zg = seg(3712, 2048) + bg_ref[...]
    gates_ref[...] = 1.0 / (1.0 + jnp.exp(-zg))


def _project(x2d, g_mix, w_pad, b_gate, cos_t, sin_t, tm):
    n, d = x2d.shape
    nblk_pos = cos_t.shape[0] // tm
    row = lambda w: pl.BlockSpec((tm, w), lambda i: (i, 0))
    col = lambda w: pl.BlockSpec((w, tm), lambda i: (0, i))
    const = lambda a: _resident(a.shape, lambda i: (0,) * a.ndim)
    pos = pl.BlockSpec((tm, LANES), lambda i: (i % nblk_pos, 0))
    out_specs, out_shape = [], []
    for width, dt, transposed in [
            (512, MXU_DTYPE, False), (512, F32, False), (512, MXU_DTYPE, False), (512, F32, False),
            (512, MXU_DTYPE, True), (512, MXU_DTYPE, False), (IDX_DIM, F32, False), (IDX_DIM, MXU_DTYPE, False),
            (IDX_HEADS, F32, True), (512, MXU_DTYPE, False), (512, F32, False), (512, MXU_DTYPE, False),
            (512, F32, False), (512, MXU_DTYPE, True), (2 * d, F32, False)]:
        out_specs.append(col(width) if transposed else row(width))
        out_shape.append(jax.ShapeDtypeStruct((width, n) if transposed else (n, width), dt))
    return pl.pallas_call(
        _project_kernel,
        grid=(n // tm,),
        in_specs=[row(d), const(g_mix), const(w_pad), const(b_gate), pos, pos],
        out_specs=out_specs,
        out_shape=out_shape,
        compiler_params=_params(("parallel",)),
        name="project",
    )(x2d, g_mix, w_pad, b_gate, cos_t, sin_t)


def _admissible(i, tq, tk, sk, causal, n_valid):
    if causal:
        nkt = (i * tq + tq + tk - 1) // tk
        nfull = (i * tq + CHUNK) // tk
        q = lax.broadcasted_iota(jnp.int32, (1, tq), 1)
        adm = ((i * tq + q) // CHUNK + 1) * CHUNK
    else:
        nkt = sk // tk
        nfull = n_valid // tk
        adm = jnp.full((1, tq), n_valid, jnp.int32)
    return nkt, nfull, adm


def _tree_sum(parts):
    while len(parts) > 1:
        parts = [a + b for a, b in zip(parts[0::2], parts[1::2])] + parts[len(parts) // 2 * 2:]
    return parts[0]


def _flash_update(s_ref, g, tile_max, m, l, acc_ref, p_ref, vt):
    tk = s_ref.shape[1]
    mn = jnp.maximum(m, tile_max)
    alpha = jnp.exp2(m - mn)
    partial = []
    for r in range(tk // SOFTMAX_ROWS):
        rows = slice(r * SOFTMAX_ROWS, (r + 1) * SOFTMAX_ROWS)
        p = jnp.exp2(s_ref[g, rows, :] - mn)
        partial.append(_tree_sum([p[t * 8:(t + 1) * 8] for t in range(SOFTMAX_ROWS // 8)]))
        p_ref[rows, :] = p.astype(p_ref.dtype)
    l = alpha * l + jnp.sum(_tree_sum(partial), axis=0, keepdims=True)
    acc_ref[g] = alpha * acc_ref[g] + jnp.dot(vt, p_ref[...], preferred_element_type=F32)
    return mn, l


def _padded_queries(q_ref, qpad_ref, nhead):
    qt = q_ref[0].astype(F32).T
    upper = lax.broadcasted_iota(jnp.int32, (LANES, qt.shape[1]), 0) < 64
    for j in range(nhead // 2):
        pair = qt[j * LANES:(j + 1) * LANES, :]
        both = jnp.concatenate([jnp.where(upper, pair, 0.0), jnp.where(upper, 0.0, pair)], axis=1)
        qpad_ref[j] = both.astype(qpad_ref.dtype)


def _key_to_float(key):
    bits = jnp.where(key >= 0, key, key ^ jnp.int32(0x7FFFFFFF))
    return lax.bitcast_convert_type(bits, F32)


def _float_to_key(x):
    bits = lax.bitcast_convert_type(x, jnp.int32)
    return bits ^ (lax.shift_right_arithmetic(bits, 31) & jnp.int32(0x7FFFFFFF))


def _key16_to_float(key16):
    bits16 = jnp.where(key16 >= 0, key16, key16 ^ jnp.int32(0x7FFF)) & jnp.int32(0xFFFF)
    return lax.bitcast_convert_type(lax.shift_left(bits16, 16), F32)


def _nan_to_neg_inf(x):
    return jnp.where(x != x, -jnp.inf, x)


def _dsa_kernel(aq_ref, iq_ref, iwt_ref, ik_ref, ak_ref, avt_ref, tri_ref, o_ref,
                sc_ref, hi_ref, iqt_ref, qpad_ref, s_ref, p0_ref, p1_ref, acc_ref,
                *, tq, tk, sk, causal, n_valid, topk):
    i = pl.program_id(1)
    p_refs = (p0_ref, p1_ref)
    nkt, _, adm = _admissible(i, tq, tk, sk, causal, n_valid)
    iqt = iq_ref[0].astype(F32).T.astype(iqt_ref.dtype)
    for j in range(IDX_HEADS // 2):
        iqt_ref[j] = jnp.concatenate([iqt[2 * j * IDX_DIM:(2 * j + 1) * IDX_DIM, :],
                                      iqt[(2 * j + 1) * IDX_DIM:(2 * j + 2) * IDX_DIM, :]], axis=1)
    _padded_queries(aq_ref, qpad_ref, A_HEADS)
    iwt = iwt_ref[0]

    def tile_start(kt):
        return pl.multiple_of(kt * tk, tk)

    def score_tile(kt, carry):
        k0 = tile_start(kt)
        kb = ik_ref[0, pl.ds(k0, tk), :]
        acc = jnp.zeros((tk, tq), F32)
        for j in range(IDX_HEADS // 2):
            s = jnp.dot(kb, iqt_ref[j], preferred_element_type=F32)
            acc = acc + iwt[2 * j:2 * j + 1, :] * jnp.maximum(s[:, :tq], 0.0)
            acc = acc + iwt[2 * j + 1:2 * j + 2, :] * jnp.maximum(s[:, tq:], 0.0)
        kpos = k0 + lax.broadcasted_iota(jnp.int32, (tk, tq), 0)
        sc = jnp.where(kpos < adm, acc, -jnp.inf)
        sc_ref[pl.ds(k0, tk), :] = sc
        top_bits = lax.bitcast_convert_type(sc, jnp.int32) & jnp.int32(-65536)
        hi_ref[pl.ds(k0, tk), :] = lax.bitcast_convert_type(top_bits, F32).astype(HI_DTYPE)
        return carry

    lax.fori_loop(0, nkt, score_tile, 0)

    one_h, zero_h = jnp.ones((), HI_DTYPE), jnp.zeros((), HI_DTYPE)

    def fold(ref, k0, rows, hit):
        accs = [None] * COUNT_ACCUMULATORS
        tile = ref[pl.ds(k0, tk), :]
        for j in range(tk // rows):
            h = hit(tile[j * rows:(j + 1) * rows])
            a = j % COUNT_ACCUMULATORS
            accs[a] = h if accs[a] is None else accs[a] + h
        return _tree_sum(accs)

    def count_hi(c):
        def body(kt, acc):
            part = fold(hi_ref, tile_start(kt), 16, lambda t: jnp.where(t >= c, one_h, zero_h))
            return acc + part.astype(F32)
        acc = lax.fori_loop(0, nkt, body, jnp.zeros((16, tq), F32))
        return jnp.sum(acc, axis=0, keepdims=True)

    def count(pred):
        def body(kt, acc):
            return acc + fold(sc_ref, tile_start(kt), 8, lambda t: jnp.where(pred(t), 1.0, 0.0))
        acc = lax.fori_loop(0, nkt, body, jnp.zeros((8, tq), F32))
        return jnp.sum(acc, axis=0, keepdims=True)

    def hi_step(b, carry):
        u, cnt = carry
        cand = u | lax.shift_left(jnp.int32(1), 15 - b)
        c = count_hi(_key16_to_float(cand - 32768).astype(HI_DTYPE))
        ok = c >= topk
        return jnp.where(ok, cand, u), jnp.where(ok, c, cnt)

    scanned = jnp.zeros((1, tq), F32) + jnp.asarray(nkt * tk, F32)
    u, cnt = lax.fori_loop(0, 16, hi_step, (jnp.zeros((1, tq), jnp.int32), scanned))
    thr_hi = _nan_to_neg_inf(_key16_to_float(u - 32768))
    key0 = _float_to_key(thr_hi) & jnp.int32(-65536)

    def lo_cond(carry):
        b, _, cnt = carry
        return (b < 16) & _any(cnt != topk)

    def lo_step(carry):
        b, key, cnt = carry
        cand = key | lax.shift_left(jnp.int32(1), 15 - b)
        cf = _key_to_float(cand)
        c = count(lambda t: t >= cf)
        ok = c >= topk
        return b + 1, jnp.where(ok, cand, key), jnp.where(ok, c, cnt)

    _, key, cnt = lax.while_loop(lo_cond, lo_step, (jnp.int32(0), key0, cnt))
    thr = _key_to_float(key)
    thr = jnp.where(jnp.abs(thr) < jnp.finfo(F32).tiny, 0.0, thr)
    thr = _nan_to_neg_inf(thr)
    has_ties = _any((cnt != topk) & (thr > -jnp.inf))
    need = lax.cond(has_ties, lambda: topk - count(lambda t: t > thr), lambda: jnp.zeros((1, tq), F32))

    def bias_plain(sc, tie_run):
        return jnp.where((sc >= thr) & (sc > -jnp.inf), 0.0, -jnp.inf), tie_run

    def bias_ties(sc, tie_run):
        eq = sc == thr
        eqb = jnp.where(eq, 1.0, 0.0).astype(MXU_DTYPE)
        prefix = jnp.dot(tri_ref[...], eqb, preferred_element_type=F32) + tie_run
        sel = ((sc > thr) | (eq & (prefix <= need))) & (sc > -jnp.inf)
        return jnp.where(sel, 0.0, -jnp.inf), prefix[tk - 1:tk, :]

    def attend_tile(kt, carry):
        tie_run, ms, ls = carry
        k0 = tile_start(kt)
        bias, tie_run = lax.cond(has_ties, bias_ties, bias_plain, sc_ref[pl.ds(k0, tk), :], tie_run)
        ms, ls, tile_max = list(ms), list(ls), []
        for j in range(A_HEADS // 2):
            kp = ak_ref[0, pl.ds(k0, tk), j * LANES:(j + 1) * LANES]
            s = jnp.dot(kp, qpad_ref[j], preferred_element_type=F32)
            for c in range(2):
                sh = s[:, c * tq:(c + 1) * tq] + bias
                s_ref[2 * j + c] = sh
                tile_max.append(jnp.max(sh, axis=0, keepdims=True))
        for h in range(A_HEADS):
            vt = avt_ref[h * A_HEAD_DIM:(h + 1) * A_HEAD_DIM, pl.ds(k0, tk)]
            ms[h], ls[h] = _flash_update(s_ref, h, tile_max[h], ms[h], ls[h], acc_ref, p_refs[h % 2], vt)
        return tie_run, tuple(ms), tuple(ls)

    acc_ref[...] = jnp.zeros_like(acc_ref)
    init = (jnp.zeros((1, tq), F32),
            tuple(jnp.full((1, tq), NEG_BIG, F32) for _ in range(A_HEADS)),
            tuple(jnp.zeros((1, tq), F32) for _ in range(A_HEADS)))
    _, _, ls = lax.fori_loop(0, nkt, attend_tile, init)
    ot = jnp.concatenate([acc_ref[h] / ls[h] for h in range(A_HEADS)], axis=0)
    o_ref[0] = ot.T.astype(o_ref.dtype)


def _dsa(aq, iq, iwt, ik, ak, avt, *, tq, tk, causal, n_valid, topk):
    b, s, _ = aq.shape
    sk = ak.shape[1]
    assert sk % tk == 0 and s % tq == 0 and tk >= TOPK_MAX and tk % LANES == 0 and tq % LANES == 0
    assert tk // 16 <= 256, "per-position hit counts must stay exact in HI_DTYPE"
    assert not causal or (tq % CHUNK == 0 and sk == s)
    tri =(jnp.arange(tk)[None, :] <= jnp.arange(tk)[:, None]).astype(MXU_DTYPE)
    qspec = pl.BlockSpec((1, tq, 512), lambda bi, i: (bi, i, 0))
    kspec = lambda w: _resident((1, sk, w), lambda bi, i: (bi, 0, 0))
    kern = functools.partial(_dsa_kernel, tq=tq, tk=tk, sk=sk, causal=causal, n_valid=n_valid, topk=topk)
    return pl.pallas_call(
        kern,
        grid=(b, s // tq),
        in_specs=[qspec, qspec, pl.BlockSpec((1, IDX_HEADS, tq), lambda bi, i: (bi, 0, i)),
                  kspec(IDX_DIM), kspec(512), _resident((512, sk), lambda bi, i: (0, bi)),
                  _resident((tk, tk), lambda bi, i: (0, 0))],
        out_specs=qspec,
        out_shape=jax.ShapeDtypeStruct((b, s, 512), MXU_DTYPE),
        scratch_shapes=[pltpu.VMEM((sk, tq), F32), pltpu.VMEM((sk, tq), HI_DTYPE),
                        pltpu.VMEM((IDX_HEADS // 2, IDX_DIM, 2 * tq), MXU_DTYPE),
                        pltpu.VMEM((A_HEADS // 2, LANES, 2 * tq), MXU_DTYPE),
                        pltpu.VMEM((A_HEADS, tk, tq), F32),
                        pltpu.VMEM((tk, tq), MXU_DTYPE), pltpu.VMEM((tk, tq), MXU_DTYPE),
                        pltpu.VMEM((A_HEADS, A_HEAD_DIM, tq), F32)],
        compiler_params=_params(("parallel", "parallel")),
        name="dsa",
    )(aq, iq, iwt, ik, ak, avt, tri)


def _diff_kernel(bq_ref, bk_ref, bvt_ref, lq1_ref, lk1_ref, lq2_ref, lk2_ref, gsub_ref, o_ref,
                 qpad_ref, s_ref, p0_ref, p1_ref, acc_ref, *, tq, tk, sk, causal, n_valid, lam_init):
    i = pl.program_id(1)
    p_refs = (p0_ref, p1_ref)
    nkt, nfull, adm = _admissible(i, tq, tk, sk, causal, n_valid)
    lam = (jnp.exp(jnp.sum(lq1_ref[...] * lk1_ref[...], axis=1, keepdims=True))
           - jnp.exp(jnp.sum(lq2_ref[...] * lk2_ref[...], axis=1, keepdims=True)) + lam_init)
    nstream = 2 * B_HEADS
    _padded_queries(bq_ref, qpad_ref, nstream)

    def tile(kt, carry, masked):
        ms, ls = (list(c) for c in carry)
        tile_max = []
        k0 = pl.multiple_of(kt * tk, tk)
        if masked:
            kpos = k0 + lax.broadcasted_iota(jnp.int32, (tk, tq), 0)
            bias = jnp.where(kpos < adm, 0.0, -jnp.inf)
        for h in range(B_HEADS):
            s = jnp.dot(bk_ref[0, pl.ds(k0, tk), h * LANES:(h + 1) * LANES], qpad_ref[h], preferred_element_type=F32)
            for c in range(2):
                sc = s[:, c * tq:(c + 1) * tq]
                sc = sc + bias if masked else sc
                s_ref[2 * h + c] = sc
                tile_max.append(jnp.max(sc, axis=0, keepdims=True))
        for g in range(nstream):
            h = g // 2
            vt = bvt_ref[h * B_V_DIM:(h + 1) * B_V_DIM, pl.ds(k0, tk)]
            ms[g], ls[g] = _flash_update(s_ref, g, tile_max[g], ms[g], ls[g], acc_ref, p_refs[g % 2], vt)
        return tuple(ms), tuple(ls)

    acc_ref[...] = jnp.zeros_like(acc_ref)
    carry = (tuple(jnp.full((1, tq), NEG_BIG, F32) for _ in range(nstream)),
             tuple(jnp.zeros((1, tq), F32) for _ in range(nstream)))
    carry = lax.fori_loop(0, nfull, functools.partial(tile, masked=False), carry)
    _, ls = lax.fori_loop(nfull, nkt, functools.partial(tile, masked=True), carry)
    outs = []
    for h in range(B_HEADS):
        o = acc_ref[2 * h] / ls[2 * h] - lam * (acc_ref[2 * h + 1] / ls[2 * h + 1])
        o = o * lax.rsqrt(jnp.mean(o * o, axis=0, keepdims=True) + EPS) * gsub_ref[...]
        outs.append(o * (1.0 - lam_init))
    o_ref[0] = jnp.concatenate(outs, axis=0).T.astype(o_ref.dtype)


def _diff(bq, bk, bvt, lq1, lk1, lq2, lk2, g_sub_col, *, tq, tk, causal, n_valid, lam_init):
    b, s, _ = bq.shape
    sk = bk.shape[1]
    assert sk % tk == 0 and s % tq == 0 and tq % LANES == 0
    assert not causal or (tq % CHUNK == 0 and sk == s)
    qspec = pl.BlockSpec((1, tq, 512), lambda bi, i: (bi, i, 0))
    vec = lambda a: _resident(a.shape, lambda bi, i: (0, 0))
    kern = functools.partial(_diff_kernel, tq=tq, tk=tk, sk=sk, causal=causal, n_valid=n_valid, lam_init=lam_init)
    return pl.pallas_call(
        kern,
        grid=(b, s // tq),
        in_specs=[qspec, _resident((1, sk, 512), lambda bi, i: (bi, 0, 0)),
                  _resident((512, sk), lambda bi, i: (0, bi)),
                  vec(lq1), vec(lk1), vec(lq2), vec(lk2), vec(g_sub_col)],
        out_specs=qspec,
        out_shape=jax.ShapeDtypeStruct((b, s, 512), MXU_DTYPE),
        scratch_shapes=[pltpu.VMEM((B_HEADS, LANES, 2 * tq), MXU_DTYPE), pltpu.VMEM((2 * B_HEADS, tk, tq), F32),
                        pltpu.VMEM((tk, tq), MXU_DTYPE), pltpu.VMEM((tk, tq), MXU_DTYPE),
                        pltpu.VMEM((2 * B_HEADS, B_V_DIM, tq), F32)],
        compiler_params=_params(("parallel", "parallel")),
        name="diff",
    )(bq, bk, bvt, lq1, lk1, lq2, lk2, g_sub_col)


def _merge_kernel(x_ref, ao_ref, bo_ref, gates_ref, wpa_ref, wpb_ref, wo_ref, gffn_ref, wpqt_ref,
                  x1_ref, htt_ref, qt_ref):
    d = x_ref.shape[1]
    gates = gates_ref[...]
    m = (gates[:, :d] * jnp.dot(ao_ref[...], wpa_ref[...], preferred_element_type=F32)
         + gates[:, d:] * jnp.dot(bo_ref[...], wpb_ref[...], preferred_element_type=F32))
    x1 = x_ref[...] + jnp.dot(m.astype(MXU_DTYPE), wo_ref[...], preferred_element_type=F32)
    x1_ref[...] = x1
    ht = _rms(x1, gffn_ref[...])
    htt_ref[...] = ht.T.astype(MXU_DTYPE)
    qt_ref[...] = lax.dot_general(wpqt_ref[...], ht.astype(MXU_DTYPE), _NT,
                                  preferred_element_type=F32).astype(qt_ref.dtype)


def _merge(x2d, ao, bo, gates, w_pa, w_pb, w_o, g_ffn, w_pq_t, tm):
    n, d = x2d.shape
    qw = w_pq_t.shape[0]
    row = lambda w: pl.BlockSpec((tm, w), lambda i: (i, 0))
    const = lambda a: _resident(a.shape, lambda i: (0, 0))
    return pl.pallas_call(
        _merge_kernel,
        grid=(n // tm,),
        in_specs=[row(d), row(512), row(512), row(2 * d), const(w_pa), const(w_pb), const(w_o),
                  const(g_ffn), const(w_pq_t)],
        out_specs=[row(d), pl.BlockSpec((d, tm), lambda i: (0, i)), pl.BlockSpec((qw, tm), lambda i: (0, i))],
        out_shape=[jax.ShapeDtypeStruct((n, d), F32), jax.ShapeDtypeStruct((d, n), MXU_DTYPE),
                   jax.ShapeDtypeStruct((qw, n), MXU_DTYPE)],
        compiler_params=_params(("parallel",)),
        name="merge",
    )(x2d, ao, bo, gates, w_pa, w_pb, w_o, g_ffn, w_pq_t)


def _top_rows(cur, k, want_rank):
    nrow, n = cur.shape
    rid = lax.broadcasted_iota(jnp.int32, (nrow, n), 0).astype(F32)
    arow = lax.broadcasted_iota(jnp.int32, (k, n), 0).astype(F32)

    def body(a, carry):
        cur, rank, vals, idxs = carry
        af = a.astype(F32)
        m = jnp.max(cur, axis=0, keepdims=True)
        idx = jnp.min(jnp.where(cur == m, rid, float(nrow)), axis=0, keepdims=True)
        hit = rid == idx
        if want_rank:
            rank = jnp.where(hit, af, rank)
        cur = jnp.where(hit, -jnp.inf, cur)
        return cur, rank, jnp.where(arow == af, m, vals), jnp.where(arow == af, idx, idxs)

    rank0 = jnp.full((nrow, n), float(k), F32) if want_rank else jnp.zeros((1, n), F32)
    init = (cur, rank0, jnp.zeros((k, n), F32), jnp.zeros((k, n), F32))
    _, rank, vals, idxs = lax.fori_loop(0, k, body, init)
    return vals, idxs, rank


def _top_distinct(curs, k):
    n = curs[0].shape[1]
    arow = lax.broadcasted_iota(jnp.int32, (k, n), 0).astype(F32)

    def body(a, carry):
        af = a.astype(F32)
        out = []
        for cur, vals in carry:
            m = jnp.max(cur, axis=0, keepdims=True)
            out.append((jnp.where(cur == m, -jnp.inf, cur), jnp.where(arow == af, m, vals)))
        return tuple(out)

    init = tuple((c, jnp.zeros((k, n), F32)) for c in curs)
    return [(vals, cur == -jnp.inf) for cur, vals in lax.fori_loop(0, k, body, init)]


def _candidate_rows(k):
    pieces, row = [], 0
    for a in range(k // 2):
        nb = -(-(k // (a + 1)) // 8) * 8
        pieces.append((row, a, 1, nb))
        row += nb
    pieces.append((row, k // 2, k - k // 2, 1))
    return pieces, row + k - k // 2


def _candidates(v1, v2, k):
    pieces, _ = _candidate_rows(k)
    parts = [v1[a:a + 1, :] + v2[0:nb, :] if na == 1 else v1[a:a + na, :] + v2[0:1, :] for _, a, na, nb in pieces]
    return jnp.concatenate(parts, axis=0)


def _taken_per_first_key(count_rows, k):
    nb = []
    for row, a, na, nbs in _candidate_rows(k)[0]:
        nb += [count_rows(row + t * nbs, row + (t + 1) * nbs) for t in range(na)]
    return nb


def _route_distinct(s1s, s2s, k):
    g = len(s1s)
    firsts = _top_distinct(list(s1s) + list(s2s), k)
    seconds = _top_distinct([_candidates(firsts[t][0], firsts[g + t][0], k) for t in range(g)], k)
    hits = lambda mask: jnp.sum(jnp.where(mask, 1.0, 0.0), axis=0, keepdims=True)
    routes, ok = [], None
    for t in range(g):
        (v1, hit1), (v2, hit2), (sv, taken) = firsts[t], firsts[g + t], seconds[t]
        nb = _taken_per_first_key(lambda lo, hi: hits(taken[lo:hi]), k)
        nbrow = jnp.zeros_like(s1s[t])
        rank2 = jnp.zeros_like(s2s[t])
        for a in range(k):
            nbrow = jnp.where(s1s[t] == v1[a:a + 1, :], nb[a], nbrow)
            rank2 = rank2 + jnp.where(v2[a:a + 1, :] > s2s[t], 1.0, 0.0)
        good = (hits(hit1) == k) & (hits(hit2) == k) & (hits(taken) == k)
        ok = good if ok is None else ok & good
        routes.append((v1[0:1, :], nbrow, v2[0:1, :], rank2, sv))
    return routes, ok


def _route_exact(s1, s2, k):
    v1, _, rank1 = _top_rows(s1, k, True)
    v2, _, rank2 = _top_rows(s2, k, True)
    sv, rows, _ = _top_rows(_candidates(v1, v2, k), k, False)
    in_rows = lambda lo, hi: jnp.sum(jnp.where((rows >= lo) & (rows < hi), 1.0, 0.0), axis=0, keepdims=True)
    nb = _taken_per_first_key(in_rows, k)
    nbrow = jnp.zeros_like(s1)
    for a in range(k):
        nbrow = jnp.where(rank1 == float(a), nb[a], nbrow)
    return v1[0:1, :], nbrow, v2[0:1, :], rank2, sv


def _peer_kernel(qt_ref, htt_ref, k1_ref, k2_ref, u_ref, vt_ref, o_ref,
                 nb_ref, c1_ref, r2_ref, e2_ref, act_ref, acc_ref, *, te):
    e = pl.program_id(1)
    half = PEER_QDIM // 2
    k = PEER_TOPK
    tm = act_ref.shape[1]
    wdt = act_ref.dtype

    @pl.when(e == 0)
    def _route():
        acc_ref[...] = jnp.zeros_like(acc_ref)

        def lane_block(c, carry):
            c0 = pl.multiple_of(c * LANES, LANES)

            def scores(h):
                r0 = pl.multiple_of(h * PEER_QDIM, PEER_QDIM)
                s1 = jnp.dot(k1_ref[...], qt_ref[pl.ds(r0, half), pl.ds(c0, LANES)], preferred_element_type=F32)
                s2 = jnp.dot(k2_ref[...], qt_ref[pl.ds(r0 + half, half), pl.ds(c0, LANES)],
                             preferred_element_type=F32)
                return s1, s2

            def store(h, s1, s2, top1, nbrow, top2, rank2, sv):
                z = jnp.sum(jnp.exp(sv - sv[0:1, :]), axis=0, keepdims=True)
                nb_ref[h, :, pl.ds(c0, LANES)] = nbrow
                c1_ref[h, :, pl.ds(c0, LANES)] = jnp.exp(s1 - top1) / z
                r2_ref[h, :, pl.ds(c0, LANES)] = rank2.astype(wdt)
                e2_ref[h, :, pl.ds(c0, LANES)] = jnp.exp(s2 - top2).astype(wdt)

            def head(h, carry2):
                s1, s2 = scores(h)
                routes, ok = _route_distinct([s1], [s2], k)
                all_ok = jnp.logical_not(_any(jnp.logical_not(ok)))
                res = lax.cond(all_ok, lambda: routes[0], lambda: _route_exact(s1, s2, k))
                store(h, s1, s2, *res)
                return carry2

            return lax.fori_loop(0, PEER_HEADS, head, carry)

        lax.fori_loop(0, tm // LANES, lane_block, 0)

    nblk = te // PEER_NKEYS
    zero = jnp.zeros((), wdt)
    cw = min(tm, NUM_MXU * MXU_WIDTH)
    for t in range(tm // cw):
        chunk = slice(t * cw, (t + 1) * cw)
        hid = jnp.dot(u_ref[...], htt_ref[:, chunk], preferred_element_type=F32)
        for j in range(nblk):
            i1 = e * nblk + j
            rows = slice(j * PEER_NKEYS, (j + 1) * PEER_NKEYS)
            nbr = [nb_ref[h, pl.ds(i1, 1), :].astype(wdt) for h in range(PEER_HEADS)]
            c1r = [c1_ref[h, pl.ds(i1, 1), :].astype(wdt) for h in range(PEER_HEADS)]
            for c in range(cw // LANES):
                cols = slice(t * cw + c * LANES, t * cw + (c + 1) * LANES)
                w = None
                for h in range(PEER_HEADS):
                    term = jnp.where(r2_ref[h, :, cols] < nbr[h][:, cols], e2_ref[h, :, cols] * c1r[h][:, cols],
                                     zero)
                    w = term if w is None else w + term
                hj = hid[rows, c * LANES:(c + 1) * LANES]
                gelu = 0.5 * hj * (1.0 + lax.erf(hj * (2.0 ** -0.5)))
                act_ref[rows, cols] = gelu.astype(wdt) * w
        acc_ref[:, chunk] += jnp.dot(vt_ref[...], act_ref[:, chunk], preferred_element_type=F32)

    @pl.when(e == pl.num_programs(1) - 1)
    def _emit():
        o_ref[...] = acc_ref[...].T


def _peer(qt, htt, sub_k1, sub_k2, u, vt, tm, te):
    d, n = htt.shape
    assert u.shape[0] == PEER_NKEYS * PEER_NKEYS and u.shape[0] % te == 0 and te % PEER_NKEYS == 0
    assert n % tm == 0 and tm % LANES == 0
    const = lambda a: _resident(a.shape, lambda i, e: (0, 0))
    route = lambda dt: pltpu.VMEM((PEER_HEADS, PEER_NKEYS, tm), dt)
    return pl.pallas_call(
        functools.partial(_peer_kernel, te=te),
        grid=(n // tm, u.shape[0] // te),
        in_specs=[pl.BlockSpec((qt.shape[0], tm), lambda i, e: (0, i)),
                  pl.BlockSpec((d, tm), lambda i, e: (0, i)),
                  const(sub_k1), const(sub_k2),
                  pl.BlockSpec((te, d), lambda i, e: (e, 0)),
                  pl.BlockSpec((d, te), lambda i, e: (0, e))],
        out_specs=pl.BlockSpec((tm, d), lambda i, e: (i, 0)),
        out_shape=jax.ShapeDtypeStruct((n, d), F32),
        scratch_shapes=[route(F32), route(F32), route(MXU_DTYPE), route(MXU_DTYPE),
                        pltpu.VMEM((te, tm), MXU_DTYPE), pltpu.VMEM((d, tm), F32)],
        compiler_params=_params(("parallel", "arbitrary")),
        name="peer",
    )(qt, htt, sub_k1, sub_k2, u, vt)


def _final_kernel(x1_ref, ff_ref, p_ref, gple_ref, wpg_ref, wple_ref, gfin_ref, y_ref):
    x2 = x1_ref[...] + ff_ref[...]
    h = _rms(x2, gple_ref[...]).astype(MXU_DTYPE)
    gate = 1.0 / (1.0 + jnp.exp(-jnp.dot(h, wpg_ref[...], preferred_element_type=F32)))
    x3 = x2 + gate * jnp.dot(p_ref[...].astype(MXU_DTYPE), wple_ref[...], preferred_element_type=F32)
    y_ref[...] = _rms(x3, gfin_ref[...])


def _final(x1, ff, p2d, g_ple, w_pg, w_ple, g_final, tm):
    n, d = x1.shape
    row = lambda w: pl.BlockSpec((tm, w), lambda i: (i, 0))
    const = lambda a: _resident(a.shape, lambda i: (0, 0))
    return pl.pallas_call(
        _final_kernel,
        grid=(n // tm,),
        in_specs=[row(d), row(d), row(p2d.shape[1]), const(g_ple), const(w_pg), const(w_ple), const(g_final)],
        out_specs=row(d),
        out_shape=jax.ShapeDtypeStruct((n, d), F32),
        compiler_params=_params(("parallel",)),
        name="final",
    )(x1, ff, p2d, g_ple, w_pg, w_ple, g_final)


def _rope_tables(pos):
    half = A_HEAD_DIM // 2
    inv = 1.0 / (ROPE_THETA ** (jnp.arange(half, dtype=F32) / half))
    ang = pos.astype(F32)[:, None] * inv[None, :]
    cos, sin = jnp.cos(ang), jnp.sin(ang)
    return jnp.tile(cos, (1, 4)), jnp.tile(jnp.concatenate([-sin, sin], axis=1), (1, 2))


def _pad_w_in(w_in):
    d = w_in.shape[0]
    head = w_in[:, :2048]
    ikw = w_in[:, 2048:2048 + IDX_DIM + IDX_HEADS]
    pad = jnp.zeros((d, LANES - IDX_DIM - IDX_HEADS), w_in.dtype)
    return jnp.concatenate([head, ikw, pad, w_in[:, 2048 + IDX_DIM + IDX_HEADS:]], axis=1).astype(MXU_DTYPE)


def _blocks(n_tokens):
    big = n_tokens >= 4096
    return dict(tm_proj=256 if big else 128, tm_merge=256 if big else 128, tm_final=256 if big else 128,
                tm_peer=512 if big else 128, te_peer=1024, tq=256, tq_diff=512, tk=512)


def _layer(x, p, caches, pos, weights, lam_init):
    (g_mix, w_pad, b_gate, lq1, lk1, lq2, lk2, g_sub_col, w_pa, w_pb, w_o, g_ffn, w_pq_t, sub_k1, sub_k2,
     peer_u, peer_vt, g_ple, w_pg, w_ple, g_final) = weights
    b, t, d = x.shape
    n = b * t
    blk = _blocks(n)
    tq, tq_diff, tk = blk["tq"], blk["tq_diff"], blk["tk"]
    x2d = x.reshape(n, d)
    cos_t, sin_t = _rope_tables(pos)
    if caches is not None:
        cos_t, sin_t = jnp.tile(cos_t, (b, 1)), jnp.tile(sin_t, (b, 1))
    (aq, akf, akb, avf, avt, iq, ikf, ikb, iwt, bq, bkf, bkb, bvf, bvt, gates) = _project(
        x2d, g_mix, w_pad, b_gate, cos_t, sin_t, blk["tm_proj"])
    r3 = lambda a: a.reshape(b, t, a.shape[-1])
    iwt3 = iwt.reshape(IDX_HEADS, b, t).transpose(1, 0, 2)
    if caches is None:
        causal, n_valid = True, t
        q3 = lambda a, tile: r3(a)
        ik3, ak3, bk3 = r3(ikb), r3(akb), r3(bkb)
    else:
        c_ak, c_av, c_ik, c_bk, c_bv = caches
        past = c_ak.shape[1]
        causal, n_valid = False, past + t
        sk = -(-n_valid // tk) * tk

        def q3(a, tile):
            return jnp.pad(r3(a), ((0, 0), (0, -t % tile), (0, 0)))

        def cat(c, new):
            c = c.reshape(b, past, -1).astype(MXU_DTYPE)
            return jnp.pad(jnp.concatenate([c, r3(new)], axis=1), ((0, 0), (0, sk - n_valid), (0, 0)))

        def cat_t(c, new_t):
            c = c.reshape(b, past, -1).astype(MXU_DTYPE).transpose(2, 0, 1)
            full = jnp.concatenate([c, new_t.reshape(-1, b, t)], axis=2)
            return jnp.pad(full, ((0, 0), (0, 0), (0, sk - n_valid))).reshape(-1, b * sk)

        ik3, ak3, bk3 = cat(c_ik, ikb), cat(c_ak, akb), cat(c_bk, bkb)
        avt, bvt = cat_t(c_av, avt), cat_t(c_bv, bvt)
        iwt3 = jnp.pad(iwt3, ((0, 0), (0, 0), (0, -t % tq)))
    topk = min(TOPK_MAX, n_valid // 4)
    a_o = _dsa(q3(aq, tq), q3(iq, tq), iwt3, ik3, ak3, avt, tq=tq, tk=tk, causal=causal, n_valid=n_valid, topk=topk)
    b_o = _diff(q3(bq, tq_diff), bk3, bvt, lq1, lk1, lq2, lk2, g_sub_col,
                tq=tq_diff, tk=tk, causal=causal, n_valid=n_valid, lam_init=lam_init)
    a_o, b_o = a_o[:, :t].reshape(n, -1), b_o[:, :t].reshape(n, -1)
    x1, htt, qt = _merge(x2d, a_o, b_o, gates, w_pa, w_pb, w_o, g_ffn, w_pq_t, blk["tm_merge"])
    ff = _peer(qt, htt, sub_k1, sub_k2, peer_u, peer_vt, blk["tm_peer"], blk["te_peer"])
    y = _final(x1, ff, p.reshape(n, -1), g_ple, w_pg, w_ple, g_final, blk["tm_final"])
    new = (akf.reshape(b, t, A_HEADS, A_HEAD_DIM), avf.reshape(b, t, A_HEADS, A_HEAD_DIM),
           ikf.reshape(b, t, IDX_DIM), bkf.reshape(b, t, B_HEADS, 2, B_HALF_DIM),
           bvf.reshape(b, t, B_HEADS, B_V_DIM))
    return y.reshape(b, t, d), new


def kernel(x_prompt, x_sample, cache_dsa_k, cache_dsa_v, cache_idx_k, cache_diff_k, cache_diff_v,
           p_prompt, p_sample, g_mix, w_in, b_gate, lam_q1, lam_k1, lam_q2, lam_k2, g_sub,
           w_pa, w_pb, w_o, g_ffn, w_pq, sub_k1, sub_k2, peer_u, peer_v, g_ple, w_pg, w_ple, g_final):
    depth = w_in.shape[0]
    assert depth == 1, "final norm is fused into the layer's last stage"
    i = 0
    lam_init = 0.8 - 0.6 * math.exp(-0.3 * i)
    c = lambda a: a.astype(MXU_DTYPE)
    vec = lambda a: a.reshape(1, -1)
    weights = (vec(g_mix[i]), _pad_w_in(w_in[i]), vec(b_gate[i]), vec(lam_q1[i]), vec(lam_k1[i]),
               vec(lam_q2[i]), vec(lam_k2[i]), g_sub[i].reshape(-1, 1), c(w_pa[i]), c(w_pb[i]), c(w_o[i]),
               vec(g_ffn[i]), c(w_pq[i].T), c(sub_k1[i]), c(sub_k2[i]), c(peer_u[i]), c(peer_v[i].T),
               vec(g_ple[i]), c(w_pg[i]), c(w_ple[i]), vec(g_final))
    past = cache_dsa_k.shape[2]
    pos_p = jnp.arange(x_prompt.shape[1])
    pos_s = past + jnp.arange(x_sample.shape[1])
    y_p, new_p = _layer(x_prompt, p_prompt[i], None, pos_p, weights, lam_init)
    caches = (cache_dsa_k[i], cache_dsa_v[i], cache_idx_k[i], cache_diff_k[i], cache_diff_v[i])
    y_s, new_s = _layer(x_sample, p_sample[i], caches, pos_s, weights, lam_init)
    stack = lambda a: a[None]
    return (y_p, y_s) + tuple(stack(a) for a in new_p) + tuple(stack(a) for a in new_s)
```

```python
import functools
import math

import jax
import jax.numpy as jnp
from jax import lax
from jax.experimental import pallas as pl
from jax.experimental.pallas import tpu as pltpu

F32 = jnp.float32
MXU_DTYPE = jnp.bfloat16
HI_DTYPE = jnp.bfloat16

EPS = 1e-6
CHUNK = 64
ROPE_THETA = 10000.0
LANES = 128
MXU_WIDTH = 256
NUM_MXU = 2
COUNT_ACCUMULATORS = 4
SOFTMAX_ROWS = 64
A_HEADS = 8
A_HEAD_DIM = 64
IDX_HEADS = 8
IDX_DIM = 64
TOPK_MAX = 256
B_HEADS = 4
B_HALF_DIM = 64
B_V_DIM = 128
PEER_HEADS = 8
PEER_NKEYS = 128
PEER_QDIM = 256
PEER_TOPK = 16
NEG_BIG = -1e30
LOG2E = math.log2(math.e)
VMEM_LIMIT = 56 * 1024 * 1024

_NT = (((1,), (1,)), ((), ()))


def _params(sem):
    return pltpu.CompilerParams(dimension_semantics=sem, vmem_limit_bytes=VMEM_LIMIT)


def _rms(x, g):
    return x * lax.rsqrt(jnp.mean(x * x, axis=-1, keepdims=True) + EPS) * g


def _resident(shape, index_map):
    return pl.BlockSpec(shape, index_map, pipeline_mode=pl.Buffered(1))


def _any(pred):
    return jnp.max(jnp.where(pred, 1.0, 0.0)) > 0.5


def _project_kernel(x_ref, g_ref, w_ref, bg_ref, cos_ref, sin_ref,
                    aq_ref, akf_ref, akb_ref, avf_ref, avt_ref, iq_ref, ikf_ref, ikb_ref, iwt_ref,
                    bq_ref, bkf_ref, bkb_ref, bvf_ref, bvt_ref, gates_ref):
    x = x_ref[...]
    h = _rms(x, g_ref[...]).astype(MXU_DTYPE)
    cos = cos_ref[...]
    sin = sin_ref[...]
    lane = lax.broadcasted_iota(jnp.int32, cos.shape, 1)
    first_half = (lane % 64) < 32

    def seg(c0, n):
        return jnp.dot(h, w_ref[:, c0:c0 + n], preferred_element_type=F32)

    def rope_slab(z):
        rot = jnp.where(first_half, pltpu.roll(z, 96, 1), pltpu.roll(z, 32, 1))
        return z * cos + rot * sin

    def roped(c0, n, scale, outs):
        z = seg(c0, n)
        for j in range(n // LANES):
            r = rope_slab(z[:, j * LANES:(j + 1) * LANES])
            if scale != 1.0:
                r = r * scale
            for ref in outs:
                ref[:, j * LANES:(j + 1) * LANES] = r.astype(ref.dtype)

    def value(c0, n, f_ref, t_ref):
        z = seg(c0, n)
        f_ref[...] = z
        t_ref[...] = z.T.astype(t_ref.dtype)

    roped(0, 512, A_HEAD_DIM ** -0.5 * LOG2E, [aq_ref])
    roped(512, 512, 1.0, [akf_ref, akb_ref])
    value(1024, 512, avf_ref, avt_ref)
    roped(1536, 512, 1.0, [iq_ref])
    zi = seg(2048, 128)
    ik = rope_slab(zi)[:, :IDX_DIM]
    ikf_ref[...] = ik
    ikb_ref[...] = ik.astype(MXU_DTYPE)
    iwt_ref[...] = zi.T[IDX_DIM:IDX_DIM + IDX_HEADS, :]
    roped(2176, 512, B_HALF_DIM ** -0.5 * LOG2E, [bq_ref])
    roped(2688, 512, 1.0, [bkf_ref, bkb_ref])
    value(3200, 512, bvf_ref, bvt_ref)
    zg = seg(3712, 2048) + bg_ref[...]
    gates_ref[...] = 1.0 / (1.0 + jnp.exp(-zg))


def _project(x2d, g_mix, w_pad, b_gate, cos_t, sin_t, tm):
    n, d = x2d.shape
    nblk_pos = cos_t.shape[0] // tm
    row = lambda w: pl.BlockSpec((tm, w), lambda i: (i, 0))
    col = lambda w: pl.BlockSpec((w, tm), lambda i: (0, i))
    const = lambda a: _resident(a.shape, lambda i: (0,) * a.ndim)
    pos = pl.BlockSpec((tm, LANES), lambda i: (i % nblk_pos, 0))
    out_specs, out_shape = [], []
    for width, dt, transposed in [
            (512, MXU_DTYPE, False), (512, F32, False), (512, MXU_DTYPE, False), (512, F32, False),
            (512, MXU_DTYPE, True), (512, MXU_DTYPE, False), (IDX_DIM, F32, False), (IDX_DIM, MXU_DTYPE, False),
            (IDX_HEADS, F32, True), (512, MXU_DTYPE, False), (512, F32, False), (512, MXU_DTYPE, False),
            (512, F32, False), (512, MXU_DTYPE, True), (2 * d, F32, False)]:
        out_specs.append(col(width) if transposed else row(width))
        out_shape.append(jax.ShapeDtypeStruct((width, n) if transposed else (n, width), dt))
    return pl.pallas_call(
        _project_kernel,
        grid=(n // tm,),
        in_specs=[row(d), const(g_mix), const(w_pad), const(b_gate), pos, pos],
        out_specs=out_specs,
        out_shape=out_shape,
        compiler_params=_params(("parallel",)),
        name="project",
    )(x2d, g_mix, w_pad, b_gate, cos_t, sin_t)


def _admissible(i, tq, tk, sk, causal, n_valid):
    if causal:
        nkt = (i * tq + tq + tk - 1) // tk
        nfull = (i * tq + CHUNK) // tk
        q = lax.broadcasted_iota(jnp.int32, (1, tq), 1)
        adm = ((i * tq + q) // CHUNK + 1) * CHUNK
    else:
        nkt = sk // tk
        nfull = n_valid // tk
        adm = jnp.full((1, tq), n_valid, jnp.int32)
    return nkt, nfull, adm


def _tree_sum(parts):
    while len(parts) > 1:
        parts = [a + b for a, b in zip(parts[0::2], parts[1::2])] + parts[len(parts) // 2 * 2:]
    return parts[0]


def _flash_update(s_ref, g, tile_max, m, l, acc_ref, p_ref, vt):
    tk = s_ref.shape[1]
    mn = jnp.maximum(m, tile_max)
    alpha = jnp.exp2(m - mn)
    partial = []
    for r in range(tk // SOFTMAX_ROWS):
        rows = slice(r * SOFTMAX_ROWS, (r + 1) * SOFTMAX_ROWS)
        p = jnp.exp2(s_ref[g, rows, :] - mn)
        partial.append(_tree_sum([p[t * 8:(t + 1) * 8] for t in range(SOFTMAX_ROWS // 8)]))
        p_ref[rows, :] = p.astype(p_ref.dtype)
    l = alpha * l + jnp.sum(_tree_sum(partial), axis=0, keepdims=True)
    acc_ref[g] = alpha * acc_ref[g] + jnp.dot(vt, p_ref[...], preferred_element_type=F32)
    return mn, l


def _padded_queries(q_ref, qpad_ref, nhead):
    qt = q_ref[0].astype(F32).T
    upper = lax.broadcasted_iota(jnp.int32, (LANES, qt.shape[1]), 0) < 64
    for j in range(nhead // 2):
        pair = qt[j * LANES:(j + 1) * LANES, :]
        both = jnp.concatenate([jnp.where(upper, pair, 0.0), jnp.where(upper, 0.0, pair)], axis=1)
        qpad_ref[j] = both.astype(qpad_ref.dtype)


def _key_to_float(key):
    bits = jnp.where(key >= 0, key, key ^ jnp.int32(0x7FFFFFFF))
    return lax.bitcast_convert_type(bits, F32)


def _float_to_key(x):
    bits = lax.bitcast_convert_type(x, jnp.int32)
    return bits ^ (lax.shift_right_arithmetic(bits, 31) & jnp.int32(0x7FFFFFFF))


def _key16_to_float(key16):
    bits16 = jnp.where(key16 >= 0, key16, key16 ^ jnp.int32(0x7FFF)) & jnp.int32(0xFFFF)
    return lax.bitcast_convert_type(lax.shift_left(bits16, 16), F32)


def _nan_to_neg_inf(x):
    return jnp.where(x != x, -jnp.inf, x)


def _dsa_kernel(aq_ref, iq_ref, iwt_ref, ik_ref, ak_ref, avt_ref, tri_ref, o_ref,
                sc_ref, hi_ref, iqt_ref, qpad_ref, s_ref, p0_ref, p1_ref, acc_ref,
                *, tq, tk, sk, causal, n_valid, topk):
    i = pl.program_id(1)
    p_refs = (p0_ref, p1_ref)
    nkt, _, adm = _admissible(i, tq, tk, sk, causal, n_valid)
    iqt = iq_ref[0].astype(F32).T.astype(iqt_ref.dtype)
    for j in range(IDX_HEADS // 2):
        iqt_ref[j] = jnp.concatenate([iqt[2 * j * IDX_DIM:(2 * j + 1) * IDX_DIM, :],
                                      iqt[(2 * j + 1) * IDX_DIM:(2 * j + 2) * IDX_DIM, :]], axis=1)
    _padded_queries(aq_ref, qpad_ref, A_HEADS)
    iwt = iwt_ref[0]

    def tile_start(kt):
        return pl.multiple_of(kt * tk, tk)

    def score_tile(kt, carry):
        k0 = tile_start(kt)
        kb = ik_ref[0, pl.ds(k0, tk), :]
        acc = jnp.zeros((tk, tq), F32)
        for j in range(IDX_HEADS // 2):
            s = jnp.dot(kb, iqt_ref[j], preferred_element_type=F32)
            acc = acc + iwt[2 * j:2 * j + 1, :] * jnp.maximum(s[:, :tq], 0.0)
            acc = acc + iwt[2 * j + 1:2 * j + 2, :] * jnp.maximum(s[:, tq:], 0.0)
        kpos = k0 + lax.broadcasted_iota(jnp.int32, (tk, tq), 0)
        sc = jnp.where(kpos < adm, acc, -jnp.inf)
        sc_ref[pl.ds(k0, tk), :] = sc
        top_bits = lax.bitcast_convert_type(sc, jnp.int32) & jnp.int32(-65536)
        hi_ref[pl.ds(k0, tk), :] = lax.bitcast_convert_type(top_bits, F32).astype(HI_DTYPE)
        return carry

    lax.fori_loop(0, nkt, score_tile, 0)

    one_h, zero_h = jnp.ones((), HI_DTYPE), jnp.zeros((), HI_DTYPE)

    def fold(ref, k0, rows, hit):
        accs = [None] * COUNT_ACCUMULATORS
        tile = ref[pl.ds(k0, tk), :]
        for j in range(tk // rows):
            h = hit(tile[j * rows:(j + 1) * rows])
            a = j % COUNT_ACCUMULATORS
            accs[a] = h if accs[a] is None else accs[a] + h
        return _tree_sum(accs)

    def count_hi(c):
        def body(kt, acc):
            part = fold(hi_ref, tile_start(kt), 16, lambda t: jnp.where(t >= c, one_h, zero_h))
            return acc + part.astype(F32)
        acc = lax.fori_loop(0, nkt, body, jnp.zeros((16, tq), F32))
        return jnp.sum(acc, axis=0, keepdims=True)

    def count(pred):
        def body(kt, acc):
            return acc + fold(sc_ref, tile_start(kt), 8, lambda t: jnp.where(pred(t), 1.0, 0.0))
        acc = lax.fori_loop(0, nkt, body, jnp.zeros((8, tq), F32))
        return jnp.sum(acc, axis=0, keepdims=True)

    def hi_step(b, carry):
        u, cnt = carry
        cand = u | lax.shift_left(jnp.int32(1), 15 - b)
        c = count_hi(_key16_to_float(cand - 32768).astype(HI_DTYPE))
        ok = c >= topk
        return jnp.where(ok, cand, u), jnp.where(ok, c, cnt)

    scanned = jnp.zeros((1, tq), F32) + jnp.asarray(nkt * tk, F32)
    u, cnt = lax.fori_loop(0, 16, hi_step, (jnp.zeros((1, tq), jnp.int32), scanned))
    thr_hi = _nan_to_neg_inf(_key16_to_float(u - 32768))
    key0 = _float_to_key(thr_hi) & jnp.int32(-65536)

    def lo_cond(carry):
        b, _, cnt = carry
        return (b < 16) & _any(cnt != topk)

    def lo_step(carry):
        b, key, cnt = carry
        cand = key | lax.shift_left(jnp.int32(1), 15 - b)
        cf = _key_to_float(cand)
        c = count(lambda t: t >= cf)
        ok = c >= topk
        return b + 1, jnp.where(ok, cand, key), jnp.where(ok, c, cnt)

    _, key, cnt = lax.while_loop(lo_cond, lo_step, (jnp.int32(0), key0, cnt))
    thr = _key_to_float(key)
    thr = jnp.where(jnp.abs(thr) < jnp.finfo(F32).tiny, 0.0, thr)
    thr = _nan_to_neg_inf(thr)
    has_ties = _any((cnt != topk) & (thr > -jnp.inf))
    need = lax.cond(has_ties, lambda: topk - count(lambda t: t > thr), lambda: jnp.zeros((1, tq), F32))

    def bias_plain(sc, tie_run):
        return jnp.where((sc >= thr) & (sc > -jnp.inf), 0.0, -jnp.inf), tie_run

    def bias_ties(sc, tie_run):
        eq = sc == thr
        eqb = jnp.where(eq, 1.0, 0.0).astype(MXU_DTYPE)
        prefix = jnp.dot(tri_ref[...], eqb, preferred_element_type=F32) + tie_run
        sel = ((sc > thr) | (eq & (prefix <= need))) & (sc > -jnp.inf)
        return jnp.where(sel, 0.0, -jnp.inf), prefix[tk - 1:tk, :]

    def attend_tile(kt, carry, bias_fn):
        tie_run, ms, ls = carry
        k0 = tile_start(kt)
        bias, tie_run = bias_fn(sc_ref[pl.ds(k0, tk), :], tie_run)
        ms, ls, tile_max = list(ms), list(ls), []
        for j in range(A_HEADS // 2):
            kp = ak_ref[0, pl.ds(k0, tk), j * LANES:(j + 1) * LANES]
            s = jnp.dot(kp, qpad_ref[j], preferred_element_type=F32)
            for c in range(2):
                sh = s[:, c * tq:(c + 1) * tq] + bias
                s_ref[2 * j + c] = sh
                tile_max.append(jnp.max(sh, axis=0, keepdims=True))
        for h in range(A_HEADS):
            vt = avt_ref[h * A_HEAD_DIM:(h + 1) * A_HEAD_DIM, pl.ds(k0, tk)]
            ms[h], ls[h] = _flash_update(s_ref, h, tile_max[h], ms[h], ls[h], acc_ref, p_refs[h % 2], vt)
        return tie_run, tuple(ms), tuple(ls)

    acc_ref[...] = jnp.zeros_like(acc_ref)
    init = (jnp.zeros((1, tq), F32),
            tuple(jnp.full((1, tq), NEG_BIG, F32) for _ in range(A_HEADS)),
            tuple(jnp.zeros((1, tq), F32) for _ in range(A_HEADS)))
    attend = lambda bias_fn: lax.fori_loop(0, nkt, functools.partial(attend_tile, bias_fn=bias_fn), init)[2]
    ls = lax.cond(has_ties, lambda: attend(bias_ties), lambda: attend(bias_plain))
    ot = jnp.concatenate([acc_ref[h] / ls[h] for h in range(A_HEADS)], axis=0)
    o_ref[0] = ot.T.astype(o_ref.dtype)


def _dsa(aq, iq, iwt, ik, ak, avt, *, tq, tk, causal, n_valid, topk):
    b, s, _ = aq.shape
    sk = ak.shape[1]
    assert sk % tk == 0 and s % tq == 0 and tk >= TOPK_MAX and tk % LANES == 0 and tq % LANES == 0
    assert tk // 16 <= 256, "per-position hit counts must stay exact in HI_DTYPE"
    assert not causal or (tq % CHUNK == 0 and sk == s)
    tri =(jnp.arange(tk)[None, :] <= jnp.arange(tk)[:, None]).astype(MXU_DTYPE)
    qspec = pl.BlockSpec((1, tq, 512), lambda bi, i: (bi, i, 0))
    kspec = lambda w: _resident((1, sk, w), lambda bi, i: (bi, 0, 0))
    kern = functools.partial(_dsa_kernel, tq=tq, tk=tk, sk=sk, causal=causal, n_valid=n_valid, topk=topk)
    return pl.pallas_call(
        kern,
        grid=(b, s // tq),
        in_specs=[qspec, qspec, pl.BlockSpec((1, IDX_HEADS, tq), lambda bi, i: (bi, 0, i)),
                  kspec(IDX_DIM), kspec(512), _resident((512, sk), lambda bi, i: (0, bi)),
                  _resident((tk, tk), lambda bi, i: (0, 0))],
        out_specs=qspec,
        out_shape=jax.ShapeDtypeStruct((b, s, 512), MXU_DTYPE),
        scratch_shapes=[pltpu.VMEM((sk, tq), F32), pltpu.VMEM((sk, tq), HI_DTYPE),
                        pltpu.VMEM((IDX_HEADS // 2, IDX_DIM, 2 * tq), MXU_DTYPE),
                        pltpu.VMEM((A_HEADS // 2, LANES, 2 * tq), MXU_DTYPE),
                        pltpu.VMEM((A_HEADS, tk, tq), F32),
                        pltpu.VMEM((tk, tq), MXU_DTYPE), pltpu.VMEM((tk, tq), MXU_DTYPE),
                        pltpu.VMEM((A_HEADS, A_HEAD_DIM, tq), F32)],
        compiler_params=_params(("parallel", "parallel")),
        name="dsa",
    )(aq, iq, iwt, ik, ak, avt, tri)


def _diff_kernel(bq_ref, bk_ref, bvt_ref, lq1_ref, lk1_ref, lq2_ref, lk2_ref, gsub_ref, o_ref,
                 qpad_ref, s_ref, p0_ref, p1_ref, acc_ref, *, tq, tk, sk, causal, n_valid, lam_init):
    i = pl.program_id(1)
    p_refs = (p0_ref, p1_ref)
    nkt, nfull, adm = _admissible(i, tq, tk, sk, causal, n_valid)
    lam = (jnp.exp(jnp.sum(lq1_ref[...] * lk1_ref[...], axis=1, keepdims=True))
           - jnp.exp(jnp.sum(lq2_ref[...] * lk2_ref[...], axis=1, keepdims=True)) + lam_init)
    nstream = 2 * B_HEADS
    _padded_queries(bq_ref, qpad_ref, nstream)

    def tile(kt, carry, masked):
        ms, ls = (list(c) for c in carry)
        tile_max = []
        k0 = pl.multiple_of(kt * tk, tk)
        if masked:
            kpos = k0 + lax.broadcasted_iota(jnp.int32, (tk, tq), 0)
            bias = jnp.where(kpos < adm, 0.0, -jnp.inf)
        for h in range(B_HEADS):
            s = jnp.dot(bk_ref[0, pl.ds(k0, tk), h * LANES:(h + 1) * LANES], qpad_ref[h], preferred_element_type=F32)
            for c in range(2):
                sc = s[:, c * tq:(c + 1) * tq]
                sc = sc + bias if masked else sc
                s_ref[2 * h + c] = sc
                tile_max.append(jnp.max(sc, axis=0, keepdims=True))
        for g in range(nstream):
            h = g // 2
            vt = bvt_ref[h * B_V_DIM:(h + 1) * B_V_DIM, pl.ds(k0, tk)]
            ms[g], ls[g] = _flash_update(s_ref, g, tile_max[g], ms[g], ls[g], acc_ref, p_refs[g % 2], vt)
        return tuple(ms), tuple(ls)

    acc_ref[...] = jnp.zeros_like(acc_ref)
    carry = (tuple(jnp.full((1, tq), NEG_BIG, F32) for _ in range(nstream)),
             tuple(jnp.zeros((1, tq), F32) for _ in range(nstream)))
    carry = lax.fori_loop(0, nfull, functools.partial(tile, masked=False), carry)
    _, ls = lax.fori_loop(nfull, nkt, functools.partial(tile, masked=True), carry)
    outs = []
    for h in range(B_HEADS):
        o = acc_ref[2 * h] / ls[2 * h] - lam * (acc_ref[2 * h + 1] / ls[2 * h + 1])
        o = o * lax.rsqrt(jnp.mean(o * o, axis=0, keepdims=True) + EPS) * gsub_ref[...]
        outs.append(o * (1.0 - lam_init))
    o_ref[0] = jnp.concatenate(outs, axis=0).T.astype(o_ref.dtype)


def _diff(bq, bk, bvt, lq1, lk1, lq2, lk2, g_sub_col, *, tq, tk, causal, n_valid, lam_init):
    b, s, _ = bq.shape
    sk = bk.shape[1]
    assert sk % tk == 0 and s % tq == 0 and tq % LANES == 0
    assert not causal or (tq % CHUNK == 0 and sk == s)
    qspec = pl.BlockSpec((1, tq, 512), lambda bi, i: (bi, i, 0))
    vec = lambda a: _resident(a.shape, lambda bi, i: (0, 0))
    kern = functools.partial(_diff_kernel, tq=tq, tk=tk, sk=sk, causal=causal, n_valid=n_valid, lam_init=lam_init)
    return pl.pallas_call(
        kern,
        grid=(b, s // tq),
        in_specs=[qspec, _resident((1, sk, 512), lambda bi, i: (bi, 0, 0)),
                  _resident((512, sk), lambda bi, i: (0, bi)),
                  vec(lq1), vec(lk1), vec(lq2), vec(lk2), vec(g_sub_col)],
        out_specs=qspec,
        out_shape=jax.ShapeDtypeStruct((b, s, 512), MXU_DTYPE),
        scratch_shapes=[pltpu.VMEM((B_HEADS, LANES, 2 * tq), MXU_DTYPE), pltpu.VMEM((2 * B_HEADS, tk, tq), F32),
                        pltpu.VMEM((tk, tq), MXU_DTYPE), pltpu.VMEM((tk, tq), MXU_DTYPE),
                        pltpu.VMEM((2 * B_HEADS, B_V_DIM, tq), F32)],
        compiler_params=_params(("parallel", "parallel")),
        name="diff",
    )(bq, bk, bvt, lq1, lk1, lq2, lk2, g_sub_col)


def _merge_kernel(x_ref, ao_ref, bo_ref, gates_ref, wpa_ref, wpb_ref, wo_ref, gffn_ref, wpqt_ref,
                  x1_ref, htt_ref, qt_ref):
    d = x_ref.shape[1]
    gates = gates_ref[...]
    m = (gates[:, :d] * jnp.dot(ao_ref[...], wpa_ref[...], preferred_element_type=F32)
         + gates[:, d:] * jnp.dot(bo_ref[...], wpb_ref[...], preferred_element_type=F32))
    x1 = x_ref[...] + jnp.dot(m.astype(MXU_DTYPE), wo_ref[...], preferred_element_type=F32)
    x1_ref[...] = x1
    ht = _rms(x1, gffn_ref[...])
    htt_ref[...] = ht.T.astype(MXU_DTYPE)
    qt_ref[...] = lax.dot_general(wpqt_ref[...], ht.astype(MXU_DTYPE), _NT,
                                  preferred_element_type=F32).astype(qt_ref.dtype)


def _merge(x2d, ao, bo, gates, w_pa, w_pb, w_o, g_ffn, w_pq_t, tm):
    n, d = x2d.shape
    qw = w_pq_t.shape[0]
    row = lambda w: pl.BlockSpec((tm, w), lambda i: (i, 0))
    const = lambda a: _resident(a.shape, lambda i: (0, 0))
    return pl.pallas_call(
        _merge_kernel,
        grid=(n // tm,),
        in_specs=[row(d), row(512), row(512), row(2 * d), const(w_pa), const(w_pb), const(w_o),
                  const(g_ffn), const(w_pq_t)],
        out_specs=[row(d), pl.BlockSpec((d, tm), lambda i: (0, i)), pl.BlockSpec((qw, tm), lambda i: (0, i))],
        out_shape=[jax.ShapeDtypeStruct((n, d), F32), jax.ShapeDtypeStruct((d, n), MXU_DTYPE),
                   jax.ShapeDtypeStruct((qw, n), MXU_DTYPE)],
        compiler_params=_params(("parallel",)),
        name="merge",
    )(x2d, ao, bo, gates, w_pa, w_pb, w_o, g_ffn, w_pq_t)


def _top_rows(cur, k, want_rank):
    nrow, n = cur.shape
    rid = lax.broadcasted_iota(jnp.int32, (nrow, n), 0).astype(F32)
    arow = lax.broadcasted_iota(jnp.int32, (k, n), 0).astype(F32)

    def body(a, carry):
        cur, rank, vals, idxs = carry
        af = a.astype(F32)
        m = jnp.max(cur, axis=0, keepdims=True)
        idx = jnp.min(jnp.where(cur == m, rid, float(nrow)), axis=0, keepdims=True)
        hit = rid == idx
        if want_rank:
            rank = jnp.where(hit, af, rank)
        cur = jnp.where(hit, -jnp.inf, cur)
        return cur, rank, jnp.where(arow == af, m, vals), jnp.where(arow == af, idx, idxs)

    rank0 = jnp.full((nrow, n), float(k), F32) if want_rank else jnp.zeros((1, n), F32)
    init = (cur, rank0, jnp.zeros((k, n), F32), jnp.zeros((k, n), F32))
    _, rank, vals, idxs = lax.fori_loop(0, k, body, init)
    return vals, idxs, rank


def _top_distinct(curs, k):
    n = curs[0].shape[1]
    arow = lax.broadcasted_iota(jnp.int32, (k, n), 0).astype(F32)

    def body(a, carry):
        af = a.astype(F32)
        out = []
        for cur, vals in carry:
            m = jnp.max(cur, axis=0, keepdims=True)
            out.append((jnp.where(cur == m, -jnp.inf, cur), jnp.where(arow == af, m, vals)))
        return tuple(out)

    init = tuple((c, jnp.zeros((k, n), F32)) for c in curs)
    return [(vals, cur == -jnp.inf) for cur, vals in lax.fori_loop(0, k, body, init)]


def _candidate_rows(k):
    pieces, row = [], 0
    for a in range(k // 2):
        nb = -(-(k // (a + 1)) // 8) * 8
        pieces.append((row, a, 1, nb))
        row += nb
    pieces.append((row, k // 2, k - k // 2, 1))
    return pieces, row + k - k // 2


def _candidates(v1, v2, k):
    pieces, _ = _candidate_rows(k)
    parts = [v1[a:a + 1, :] + v2[0:nb, :] if na == 1 else v1[a:a + na, :] + v2[0:1, :] for _, a, na, nb in pieces]
    return jnp.concatenate(parts, axis=0)


def _taken_per_first_key(count_rows, k):
    nb = []
    for row, a, na, nbs in _candidate_rows(k)[0]:
        nb += [count_rows(row + t * nbs, row + (t + 1) * nbs) for t in range(na)]
    return nb


def _route_distinct(s1s, s2s, k):
    g = len(s1s)
    firsts = _top_distinct(list(s1s) + list(s2s), k)
    seconds = _top_distinct([_candidates(firsts[t][0], firsts[g + t][0], k) for t in range(g)], k)
    hits = lambda mask: jnp.sum(jnp.where(mask, 1.0, 0.0), axis=0, keepdims=True)
    routes, ok = [], None
    for t in range(g):
        (v1, hit1), (v2, hit2), (sv, taken) = firsts[t], firsts[g + t], seconds[t]
        nb = _taken_per_first_key(lambda lo, hi: hits(taken[lo:hi]), k)
        nbrow = jnp.zeros_like(s1s[t])
        rank2 = jnp.zeros_like(s2s[t])
        for a in range(k):
            nbrow = jnp.where(s1s[t] == v1[a:a + 1, :], nb[a], nbrow)
            rank2 = rank2 + jnp.where(v2[a:a + 1, :] > s2s[t], 1.0, 0.0)
        good = (hits(hit1) == k) & (hits(hit2) == k) & (hits(taken) == k)
        ok = good if ok is None else ok & good
        routes.append((v1[0:1, :], nbrow, v2[0:1, :], rank2, sv))
    return routes, ok


def _route_exact(s1, s2, k):
    v1, _, rank1 = _top_rows(s1, k, True)
    v2, _, rank2 = _top_rows(s2, k, True)
    sv, rows, _ = _top_rows(_candidates(v1, v2, k), k, False)
    in_rows = lambda lo, hi: jnp.sum(jnp.where((rows >= lo) & (rows < hi), 1.0, 0.0), axis=0, keepdims=True)
    nb = _taken_per_first_key(in_rows, k)
    nbrow = jnp.zeros_like(s1)
    for a in range(k):
        nbrow = jnp.where(rank1 == float(a), nb[a], nbrow)
    return v1[0:1, :], nbrow, v2[0:1, :], rank2, sv


def _peer_kernel(qt_ref, htt_ref, k1_ref, k2_ref, u_ref, vt_ref, o_ref,
                 nb_ref, c1_ref, r2_ref, e2_ref, act_ref, acc_ref, *, te):
    e = pl.program_id(1)
    half = PEER_QDIM // 2
    k = PEER_TOPK
    tm = act_ref.shape[1]
    wdt = act_ref.dtype

    @pl.when(e == 0)
    def _route():
        acc_ref[...] = jnp.zeros_like(acc_ref)

        def lane_block(c, carry):
            c0 = pl.multiple_of(c * LANES, LANES)

            def scores(h):
                r0 = pl.multiple_of(h * PEER_QDIM, PEER_QDIM)
                s1 = jnp.dot(k1_ref[...], qt_ref[pl.ds(r0, half), pl.ds(c0, LANES)], preferred_element_type=F32)
                s2 = jnp.dot(k2_ref[...], qt_ref[pl.ds(r0 + half, half), pl.ds(c0, LANES)],
                             preferred_element_type=F32)
                return s1, s2

            def store(h, s1, s2, top1, nbrow, top2, rank2, sv):
                z = jnp.sum(jnp.exp(sv - sv[0:1, :]), axis=0, keepdims=True)
                nb_ref[h, :, pl.ds(c0, LANES)] = nbrow
                c1_ref[h, :, pl.ds(c0, LANES)] = 0.5 * jnp.exp(s1 - top1) / z
                r2_ref[h, :, pl.ds(c0, LANES)] = rank2.astype(wdt)
                e2_ref[h, :, pl.ds(c0, LANES)] = jnp.exp(s2 - top2).astype(wdt)

            def head(h, carry2):
                s1, s2 = scores(h)
                routes, ok = _route_distinct([s1], [s2], k)
                all_ok = jnp.logical_not(_any(jnp.logical_not(ok)))
                res = lax.cond(all_ok, lambda: routes[0], lambda: _route_exact(s1, s2, k))
                store(h, s1, s2, *res)
                return carry2

            return lax.fori_loop(0, PEER_HEADS, head, carry)

        lax.fori_loop(0, tm // LANES, lane_block, 0)

    nblk = te // PEER_NKEYS
    zero = jnp.zeros((), wdt)
    cw = min(tm, NUM_MXU * MXU_WIDTH)
    for t in range(tm // cw):
        chunk = slice(t * cw, (t + 1) * cw)
        hid = jnp.dot(u_ref[...], htt_ref[:, chunk], preferred_element_type=F32)
        for j in range(nblk):
            i1 = e * nblk + j
            rows = slice(j * PEER_NKEYS, (j + 1) * PEER_NKEYS)
            nbr = [nb_ref[h, pl.ds(i1, 1), :].astype(wdt) for h in range(PEER_HEADS)]
            c1r = [c1_ref[h, pl.ds(i1, 1), :].astype(wdt) for h in range(PEER_HEADS)]
            for c in range(cw // LANES):
                cols = slice(t * cw + c * LANES, t * cw + (c + 1) * LANES)
                w = None
                for h in range(PEER_HEADS):
                    term = jnp.where(r2_ref[h, :, cols] < nbr[h][:, cols], e2_ref[h, :, cols] * c1r[h][:, cols],
                                     zero)
                    w = term if w is None else w + term
                hj = hid[rows, c * LANES:(c + 1) * LANES]
                gelu2 = hj * (1.0 + lax.erf(hj * (2.0 ** -0.5)))
                act_ref[rows, cols] = gelu2.astype(wdt) * w
        acc_ref[:, chunk] += jnp.dot(vt_ref[...], act_ref[:, chunk], preferred_element_type=F32)

    @pl.when(e == pl.num_programs(1) - 1)
    def _emit():
        o_ref[...] = acc_ref[...].T


def _peer(qt, htt, sub_k1, sub_k2, u, vt, tm, te):
    d, n = htt.shape
    assert u.shape[0] == PEER_NKEYS * PEER_NKEYS and u.shape[0] % te == 0 and te % PEER_NKEYS == 0
    assert n % tm == 0 and tm % LANES == 0
    const = lambda a: _resident(a.shape, lambda i, e: (0, 0))
    route = lambda dt: pltpu.VMEM((PEER_HEADS, PEER_NKEYS, tm), dt)
    return pl.pallas_call(
        functools.partial(_peer_kernel, te=te),
        grid=(n // tm, u.shape[0] // te),
        in_specs=[pl.BlockSpec((qt.shape[0], tm), lambda i, e: (0, i)),
                  pl.BlockSpec((d, tm), lambda i, e: (0, i)),
                  const(sub_k1), const(sub_k2),
                  pl.BlockSpec((te, d), lambda i, e: (e, 0)),
                  pl.BlockSpec((d, te), lambda i, e: (0, e))],
        out_specs=pl.BlockSpec((tm, d), lambda i, e: (i, 0)),
        out_shape=jax.ShapeDtypeStruct((n, d), F32),
        scratch_shapes=[route(F32), route(F32), route(MXU_DTYPE), route(MXU_DTYPE),
                        pltpu.VMEM((te, tm), MXU_DTYPE), pltpu.VMEM((d, tm), F32)],
        compiler_params=_params(("parallel", "arbitrary")),
        name="peer",
    )(qt, htt, sub_k1, sub_k2, u, vt)


def _final_kernel(x1_ref, ff_ref, p_ref, gple_ref, wpg_ref, wple_ref, gfin_ref, y_ref):
    x2 = x1_ref[...] + ff_ref[...]
    h = _rms(x2, gple_ref[...]).astype(MXU_DTYPE)
    gate = 1.0 / (1.0 + jnp.exp(-jnp.dot(h, wpg_ref[...], preferred_element_type=F32)))
    x3 = x2 + gate * jnp.dot(p_ref[...].astype(MXU_DTYPE), wple_ref[...], preferred_element_type=F32)
    y_ref[...] = _rms(x3, gfin_ref[...])


def _final(x1, ff, p2d, g_ple, w_pg, w_ple, g_final, tm):
    n, d = x1.shape
    row = lambda w: pl.BlockSpec((tm, w), lambda i: (i, 0))
    const = lambda a: _resident(a.shape, lambda i: (0, 0))
    return pl.pallas_call(
        _final_kernel,
        grid=(n // tm,),
        in_specs=[row(d), row(d), row(p2d.shape[1]), const(g_ple), const(w_pg), const(w_ple), const(g_final)],
        out_specs=row(d),
        out_shape=jax.ShapeDtypeStruct((n, d), F32),
        compiler_params=_params(("parallel",)),
        name="final",
    )(x1, ff, p2d, g_ple, w_pg, w_ple, g_final)


def _rope_tables(pos):
    half = A_HEAD_DIM // 2
    inv = 1.0 / (ROPE_THETA ** (jnp.arange(half, dtype=F32) / half))
    ang = pos.astype(F32)[:, None] * inv[None, :]
    cos, sin = jnp.cos(ang), jnp.sin(ang)
    return jnp.tile(cos, (1, 4)), jnp.tile(jnp.concatenate([-sin, sin], axis=1), (1, 2))


def _pad_w_in(w_in):
    d = w_in.shape[0]
    head = w_in[:, :2048]
    ikw = w_in[:, 2048:2048 + IDX_DIM + IDX_HEADS]
    pad = jnp.zeros((d, LANES - IDX_DIM - IDX_HEADS), w_in.dtype)
    return jnp.concatenate([head, ikw, pad, w_in[:, 2048 + IDX_DIM + IDX_HEADS:]], axis=1).astype(MXU_DTYPE)


def _blocks(n_tokens):
    big = n_tokens >= 4096
    return dict(tm_proj=256 if big else 128, tm_merge=256 if big else 128, tm_final=256 if big else 128,
                tm_peer=512 if big else 128, te_peer=2048, tq=256, tq_diff=512, tk=512)


def _layer(x, p, caches, pos, weights, lam_init):
    (g_mix, w_pad, b_gate, lq1, lk1, lq2, lk2, g_sub_col, w_pa, w_pb, w_o, g_ffn, w_pq_t, sub_k1, sub_k2,
     peer_u, peer_vt, g_ple, w_pg, w_ple, g_final) = weights
    b, t, d = x.shape
    n = b * t
    blk = _blocks(n)
    tq, tq_diff, tk = blk["tq"], blk["tq_diff"], blk["tk"]
    x2d = x.reshape(n, d)
    cos_t, sin_t = _rope_tables(pos)
    if caches is not None:
        cos_t, sin_t = jnp.tile(cos_t, (b, 1)), jnp.tile(sin_t, (b, 1))
    (aq, akf, akb, avf, avt, iq, ikf, ikb, iwt, bq, bkf, bkb, bvf, bvt, gates) = _project(
        x2d, g_mix, w_pad, b_gate, cos_t, sin_t, blk["tm_proj"])
    r3 = lambda a: a.reshape(b, t, a.shape[-1])
    iwt3 = iwt.reshape(IDX_HEADS, b, t).transpose(1, 0, 2)
    if caches is None:
        causal, n_valid = True, t
        q3 = lambda a, tile: r3(a)
        ik3, ak3, bk3 = r3(ikb), r3(akb), r3(bkb)
    else:
        c_ak, c_av, c_ik, c_bk, c_bv = caches
        past = c_ak.shape[1]
        causal, n_valid = False, past + t
        sk = -(-n_valid // tk) * tk

        def q3(a, tile):
            return jnp.pad(r3(a), ((0, 0), (0, -t % tile), (0, 0)))

        def cat(c, new):
            c = c.reshape(b, past, -1).astype(MXU_DTYPE)
            return jnp.pad(jnp.concatenate([c, r3(new)], axis=1), ((0, 0), (0, sk - n_valid), (0, 0)))

        def cat_t(c, new_t):
            c = c.reshape(b, past, -1).astype(MXU_DTYPE).transpose(2, 0, 1)
            full = jnp.concatenate([c, new_t.reshape(-1, b, t)], axis=2)
            return jnp.pad(full, ((0, 0), (0, 0), (0, sk - n_valid))).reshape(-1, b * sk)

        ik3, ak3, bk3 = cat(c_ik, ikb), cat(c_ak, akb), cat(c_bk, bkb)
        avt, bvt = cat_t(c_av, avt), cat_t(c_bv, bvt)
        iwt3 = jnp.pad(iwt3, ((0, 0), (0, 0), (0, -t % tq)))
    topk = min(TOPK_MAX, n_valid // 4)
    a_o = _dsa(q3(aq, tq), q3(iq, tq), iwt3, ik3, ak3, avt, tq=tq, tk=tk, causal=causal, n_valid=n_valid, topk=topk)
    b_o = _diff(q3(bq, tq_diff), bk3, bvt, lq1, lk1, lq2, lk2, g_sub_col,
                tq=tq_diff, tk=tk, causal=causal, n_valid=n_valid, lam_init=lam_init)
    a_o, b_o = a_o[:, :t].reshape(n, -1), b_o[:, :t].reshape(n, -1)
    x1, htt, qt = _merge(x2d, a_o, b_o, gates, w_pa, w_pb, w_o, g_ffn, w_pq_t, blk["tm_merge"])
    ff = _peer(qt, htt, sub_k1, sub_k2, peer_u, peer_vt, blk["tm_peer"], blk["te_peer"])
    y = _final(x1, ff, p.reshape(n, -1), g_ple, w_pg, w_ple, g_final, blk["tm_final"])
    new = (akf.reshape(b, t, A_HEADS, A_HEAD_DIM), avf.reshape(b, t, A_HEADS, A_HEAD_DIM),
           ikf.reshape(b, t, IDX_DIM), bkf.reshape(b, t, B_HEADS, 2, B_HALF_DIM),
           bvf.reshape(b, t, B_HEADS, B_V_DIM))
    return y.reshape(b, t, d), new


def kernel(x_prompt, x_sample, cache_dsa_k, cache_dsa_v, cache_idx_k, cache_diff_k, cache_diff_v,
           p_prompt, p_sample, g_mix, w_in, b_gate, lam_q1, lam_k1, lam_q2, lam_k2, g_sub,
           w_pa, w_pb, w_o, g_ffn, w_pq, sub_k1, sub_k2, peer_u, peer_v, g_ple, w_pg, w_ple, g_final):
    depth = w_in.shape[0]
    assert depth == 1, "final norm is fused into the layer's last stage"
    i = 0
    lam_init = 0.8 - 0.6 * math.exp(-0.3 * i)
    c = lambda a: a.astype(MXU_DTYPE)
    vec = lambda a: a.reshape(1, -1)
    weights = (vec(g_mix[i]), _pad_w_in(w_in[i]), vec(b_gate[i]), vec(lam_q1[i]), vec(lam_k1[i]),
               vec(lam_q2[i]), vec(lam_k2[i]), g_sub[i].reshape(-1, 1), c(w_pa[i]), c(w_pb[i]), c(w_o[i]),
               vec(g_ffn[i]), c(w_pq[i].T), c(sub_k1[i]), c(sub_k2[i]), c(peer_u[i]), c(peer_v[i].T),
               vec(g_ple[i]), c(w_pg[i]), c(w_ple[i]), vec(g_final))
    past = cache_dsa_k.shape[2]
    pos_p = jnp.arange(x_prompt.shape[1])
    pos_s = past + jnp.arange(x_sample.shape[1])
    y_p, new_p = _layer(x_prompt, p_prompt[i], None, pos_p, weights, lam_init)
    caches = (cache_dsa_k[i], cache_dsa_v[i], cache_idx_k[i], cache_diff_k[i], cache_diff_v[i])
    y_s, new_s = _layer(x_sample, p_sample[i], caches, pos_s, weights, lam_init)
    stack = lambda a: a[None]
    return (y_p, y_s) + tuple(stack(a) for a in new_p) + tuple(stack(a) for a in new_s)
```
